```python
import math
import jax, jax.numpy as jnp
from jax import lax
import numpy as np

D_MODEL = 1024
BATCH = 4
SEQ = 8192
DEPTH = 1

GDN_HEADS = 4
GDN_HEAD_DIM = 128
GDN_WIDTH = GDN_HEADS * GDN_HEAD_DIM
CONV_WIDTH = 4
GDN_CHUNK = 64
DSA_HEADS = 4
DSA_HEAD_DIM = 128
DSA_WIDTH = DSA_HEADS * DSA_HEAD_DIM
Q_RANK = 256
KV_RANK = 256
IDX_HEADS = 16
IDX_DIM = 64
IDX_TOPK_MAX = 256
Q_BLOCK = 128
REL_BUCKETS = 32
REL_MAX_DIST = 1024
PEER_HEADS = 8
PEER_KEYS = 128
PEER_EXPERTS = PEER_KEYS * PEER_KEYS
PEER_KEY_DIM = 256
PEER_TOPK = 16
PEER_BLOCK = 128

MIX_WIDTH = GDN_WIDTH + DSA_WIDTH
IN_SPLIT_SIZES = (GDN_WIDTH, GDN_WIDTH, GDN_WIDTH, GDN_WIDTH, GDN_HEADS, GDN_HEADS,
                  Q_RANK, KV_RANK, IDX_DIM, IDX_HEADS)
IN_WIDTH = sum(IN_SPLIT_SIZES)
EPS = 1e-6
F32 = jnp.float32

kernel_name = "hybrid_gdn_dsa_peer_block"


def rms_norm(x, g):
    xf = x.astype(F32)
    y = xf * lax.rsqrt(jnp.mean(xf * xf, axis=-1, keepdims=True) + EPS)
    return (y * g.astype(F32)).astype(x.dtype)


def layer_norm(x, g, b):
    xf = x.astype(F32)
    mu = jnp.mean(xf, axis=-1, keepdims=True)
    xc = xf - mu
    y = xc * lax.rsqrt(jnp.mean(xc * xc, axis=-1, keepdims=True) + EPS)
    return (y * g.astype(F32) + b.astype(F32)).astype(x.dtype)


def l2_normalize(x):
    xf = x.astype(F32)
    return xf * lax.rsqrt(jnp.sum(xf * xf, axis=-1, keepdims=True) + EPS)


def causal_depthwise_conv(x, w):
    c = x.shape[-1]
    return lax.conv_general_dilated(
        x, w[:, None, :].astype(x.dtype), window_strides=(1,),
        padding=((CONV_WIDTH - 1, 0),), dimension_numbers=('NWC', 'WIO', 'NWC'),
        feature_group_count=c)


def gated_delta_rule(q, k, v, g, beta):
    bsz, t_len, h, dk = q.shape
    dv = v.shape[-1]
    c = GDN_CHUNK
    n = t_len // c
    q = q.astype(F32) * (dk ** -0.5)

    def chunk4(a):
        return a.astype(F32).reshape(bsz, n, c, h, a.shape[-1]).transpose(0, 3, 1, 2, 4)

    def chunk3(a):
        return a.astype(F32).reshape(bsz, n, c, h).transpose(0, 3, 1, 2)

    q, k, v = chunk4(q), chunk4(k), chunk4(v)
    g = jnp.cumsum(chunk3(g), axis=-1)
    beta = chunk3(beta)
    tril = jnp.tril(jnp.ones((c, c), dtype=bool))
    strict = jnp.tril(jnp.ones((c, c), dtype=bool), -1)
    decay = jnp.exp(jnp.where(tril, g[..., :, None] - g[..., None, :], -jnp.inf))
    k_beta = k * beta[..., None]
    v_beta = v * beta[..., None]
    a_mat = jnp.where(strict, jnp.einsum('bhncd,bhnsd->bhncs', k_beta, k) * decay, 0.0)
    eye = jnp.eye(c, dtype=F32)
    t_inv = lax.linalg.triangular_solve(a_mat + eye, jnp.broadcast_to(eye, a_mat.shape),
                                        left_side=True, lower=True, unit_diagonal=True)
    u = jnp.einsum('bhncs,bhnse->bhnce', t_inv, v_beta)
    w = jnp.einsum('bhncs,bhnsd->bhncd', t_inv, k_beta * jnp.exp(g)[..., None])
    attn = jnp.einsum('bhncd,bhnsd->bhncs', q, k) * decay

    def step(state, xs):
        q_c, k_c, u_c, w_c, g_c, a_c = xs
        v_new = u_c - jnp.einsum('bhcd,bhde->bhce', w_c, state)
        o_c = (jnp.einsum('bhcd,bhde->bhce', q_c * jnp.exp(g_c)[..., None], state)
               + jnp.einsum('bhcs,bhse->bhce', a_c, v_new))
        g_last = g_c[..., -1:]
        state = (state * jnp.exp(g_last)[..., None]
                 + jnp.einsum('bhcd,bhce->bhde', k_c * jnp.exp(g_last - g_c)[..., None], v_new))
        return state, o_c

    xs = (q.transpose(2, 0, 1, 3, 4), k.transpose(2, 0, 1, 3, 4), u.transpose(2, 0, 1, 3, 4),
          w.transpose(2, 0, 1, 3, 4), g.transpose(2, 0, 1, 3), attn.transpose(2, 0, 1, 3, 4))
    s0 = jnp.zeros((bsz, h, dk, dv), F32)
    _, o = lax.scan(step, s0, xs)
    return o.transpose(1, 0, 3, 2, 4).reshape(bsz, t_len, h, dv)


def gdn_mixer(q, k, v, z, a, b, conv_w, a_log, dt_bias, norm_g):
    bsz, t_len, _ = q.shape
    qkv = jax.nn.silu(causal_depthwise_conv(jnp.concatenate([q, k, v], axis=-1), conv_w))
    q, k, v = jnp.split(qkv, [GDN_WIDTH, 2 * GDN_WIDTH], axis=-1)
    q = l2_normalize(q.reshape(bsz, t_len, GDN_HEADS, GDN_HEAD_DIM))
    k = l2_normalize(k.reshape(bsz, t_len, GDN_HEADS, GDN_HEAD_DIM))
    v = v.reshape(bsz, t_len, GDN_HEADS, GDN_HEAD_DIM)
    g = -jnp.exp(a_log.astype(F32)) * jax.nn.softplus(a.astype(F32) + dt_bias.astype(F32))
    beta = jax.nn.sigmoid(b.astype(F32))
    o = gated_delta_rule(q, k, v, g, beta)
    o = rms_norm(o, norm_g) * jax.nn.silu(z.reshape(bsz, t_len, GDN_HEADS, GDN_HEAD_DIM).astype(F32))
    return o.reshape(bsz, t_len, GDN_WIDTH).astype(z.dtype)


def rel_bucket(dist):
    max_exact = REL_BUCKETS // 2
    n = jnp.maximum(dist, 0)
    nf = jnp.maximum(n, 1).astype(F32)
    large = max_exact + (jnp.log(nf / max_exact) / math.log(REL_MAX_DIST / max_exact)
                         * (REL_BUCKETS - max_exact)).astype(jnp.int32)
    large = jnp.minimum(large, REL_BUCKETS - 1)
    return jnp.where(n < max_exact, n, large)


def dsa_mixer(c_q, c_kv, k_idx, w_idx, q_norm_g, kv_norm_g, w_q_up, w_qidx_up, w_kv_up,
              idx_ln_g, idx_ln_b, rel_bias):
    bsz, t_len, _ = c_q.shape
    cq = rms_norm(c_q, q_norm_g)
    q = (cq @ w_q_up).reshape(bsz, t_len, DSA_HEADS, DSA_HEAD_DIM)
    q_idx = (cq @ w_qidx_up).reshape(bsz, t_len, IDX_HEADS, IDX_DIM)
    ckv = rms_norm(c_kv, kv_norm_g)
    kv = (ckv @ w_kv_up).reshape(bsz, t_len, DSA_HEADS, 2 * DSA_HEAD_DIM)
    k, v = kv[..., :DSA_HEAD_DIM], kv[..., DSA_HEAD_DIM:]
    k_idx = layer_norm(k_idx, idx_ln_g, idx_ln_b).astype(F32)
    w_idx = w_idx.astype(F32) * (IDX_HEADS ** -0.5 * IDX_DIM ** -0.5)
    topk = min(IDX_TOPK_MAX, t_len // 4)
    nb = t_len // Q_BLOCK
    scale = DSA_HEAD_DIM ** -0.5
    key_pos = jnp.arange(t_len)

    def blk(a):
        return a.reshape(bsz, nb, Q_BLOCK, *a.shape[2:]).swapaxes(0, 1)

    def attend_block(args):
        i, q_b, qi_b, wi_b = args
        t_pos = i * Q_BLOCK + jnp.arange(Q_BLOCK)
        causal = key_pos[None, :] <= t_pos[:, None]
        idx_logits = jnp.einsum('bthd,bsd->bths', qi_b.astype(F32), k_idx)
        score = jnp.einsum('bths,bth->bts', jax.nn.relu(idx_logits), wi_b)
        score = jnp.where(causal[None], score, -jnp.inf)
        _, sel = lax.top_k(score, topk)
        k_sel = jax.vmap(lambda kk, ii: kk[ii])(k, sel)
        v_sel = jax.vmap(lambda vv, ii: vv[ii])(v, sel)
        logits = jnp.einsum('bthd,btkhd->bhtk', q_b.astype(F32), k_sel.astype(F32)) * scale
        dist = t_pos[None, :, None] - sel
        bias = rel_bias[rel_bucket(dist)].astype(F32)
        logits = logits + bias.transpose(0, 3, 1, 2)
        logits = jnp.where((dist >= 0)[:, None], logits, -jnp.inf)
        p = jax.nn.softmax(logits, axis=-1)
        out = jnp.einsum('bhtk,btkhd->bthd', p, v_sel.astype(F32))
        return out.astype(q_b.dtype)

    outs = lax.map(attend_block, (jnp.arange(nb), blk(q), blk(q_idx), blk(w_idx)))
    return outs.swapaxes(0, 1).reshape(bsz, t_len, DSA_WIDTH)


def peer_ffn(h, w_query, sub_keys_1, sub_keys_2, expert_u, expert_v):
    bsz, t_len, d = h.shape
    nb = t_len // PEER_BLOCK
    n_tok = bsz * PEER_BLOCK
    half = PEER_KEY_DIM // 2
    hb = h.reshape(bsz, nb, PEER_BLOCK, d).swapaxes(0, 1).reshape(nb, n_tok, d)

    def block(xb):
        qy = (xb @ w_query).reshape(n_tok, PEER_HEADS, 2, half)
        s1 = jnp.einsum('nhd,hkd->nhk', qy[:, :, 0], sub_keys_1).astype(F32)
        s2 = jnp.einsum('nhd,hkd->nhk', qy[:, :, 1], sub_keys_2).astype(F32)
        v1, i1 = lax.top_k(s1, PEER_TOPK)
        v2, i2 = lax.top_k(s2, PEER_TOPK)
        cand = (v1[..., :, None] + v2[..., None, :]).reshape(n_tok, PEER_HEADS, PEER_TOPK * PEER_TOPK)
        cidx = (i1[..., :, None] * PEER_KEYS + i2[..., None, :]).reshape(n_tok, PEER_HEADS, PEER_TOPK * PEER_TOPK)
        top_s, pos = lax.top_k(cand, PEER_TOPK)
        eidx = jnp.take_along_axis(cidx, pos, axis=-1)
        gates = jax.nn.softmax(top_s, axis=-1)
        u = expert_u[eidx]
        vv = expert_v[eidx]
        act = jax.nn.gelu(jnp.einsum('nhkd,nd->nhk', u, xb).astype(F32), approximate=False)
        return jnp.einsum('nhk,nhkd->nd', (gates * act).astype(vv.dtype), vv)

    y = lax.map(block, hb)
    return y.reshape(nb, bsz, PEER_BLOCK, d).swapaxes(0, 1).reshape(bsz, t_len, d)


def setup_inputs(seed: int = 0) -> dict:
    key = jax.random.key(seed)
    ks = jax.random.split(key, 24)
    nrm = lambda k, shape, s: jax.random.normal(k, shape, F32) * s
    gain = lambda k, shape: 1.0 + 0.02 * jax.random.normal(k, shape, F32)
    dt = jnp.exp(jax.random.uniform(ks[4], (DEPTH, GDN_HEADS), F32, math.log(1e-3), math.log(1e-1)))
    return {
        "x": nrm(ks[0], (BATCH, SEQ, D_MODEL), 1.0),
        "ln1_g": gain(ks[1], (DEPTH, D_MODEL)),
        "w_in": nrm(ks[2], (DEPTH, D_MODEL, IN_WIDTH), D_MODEL ** -0.5),
        "conv_w": nrm(ks[3], (DEPTH, CONV_WIDTH, 3 * GDN_WIDTH), CONV_WIDTH ** -0.5),
        "a_log": jnp.log(jax.random.uniform(ks[5], (DEPTH, GDN_HEADS), F32, 1.0, 16.0)),
        "dt_bias": dt + jnp.log(-jnp.expm1(-dt)),
        "gdn_norm_g": gain(ks[6], (DEPTH, GDN_HEAD_DIM)),
        "q_norm_g": gain(ks[7], (DEPTH, Q_RANK)),
        "kv_norm_g": gain(ks[8], (DEPTH, KV_RANK)),
        "w_q_up": nrm(ks[9], (DEPTH, Q_RANK, DSA_WIDTH), Q_RANK ** -0.5),
        "w_qidx_up": nrm(ks[10], (DEPTH, Q_RANK, IDX_HEADS * IDX_DIM), Q_RANK ** -0.5),
        "w_kv_up": nrm(ks[11], (DEPTH, KV_RANK, 2 * DSA_WIDTH), KV_RANK ** -0.5),
        "idx_ln_g": gain(ks[12], (DEPTH, IDX_DIM)),
        "idx_ln_b": nrm(ks[13], (DEPTH, IDX_DIM), 0.02),
        "w_out": nrm(ks[14], (DEPTH, MIX_WIDTH, D_MODEL), MIX_WIDTH ** -0.5),
        "ln2_g": gain(ks[15], (DEPTH, D_MODEL)),
        "peer_w_query": nrm(ks[16], (DEPTH, D_MODEL, PEER_HEADS * PEER_KEY_DIM), D_MODEL ** -0.5),
        "peer_sub_keys_1": nrm(ks[17], (DEPTH, PEER_HEADS, PEER_KEYS, PEER_KEY_DIM // 2), (PEER_KEY_DIM // 2) ** -0.5),
        "peer_sub_keys_2": nrm(ks[18], (DEPTH, PEER_HEADS, PEER_KEYS, PEER_KEY_DIM // 2), (PEER_KEY_DIM // 2) ** -0.5),
        "peer_u": nrm(ks[19], (DEPTH, PEER_EXPERTS, D_MODEL), D_MODEL ** -0.5),
        "peer_v": nrm(ks[20], (DEPTH, PEER_EXPERTS, D_MODEL), PEER_HEADS ** -0.5),
        "rel_bias": nrm(ks[21], (REL_BUCKETS, DSA_HEADS), 0.5),
        "final_g": gain(ks[22], (D_MODEL,)),
    }


def reference(x, ln1_g, w_in, conv_w, a_log, dt_bias, gdn_norm_g, q_norm_g, kv_norm_g,
              w_q_up, w_qidx_up, w_kv_up, idx_ln_g, idx_ln_b, w_out, ln2_g, peer_w_query,
              peer_sub_keys_1, peer_sub_keys_2, peer_u, peer_v, rel_bias, final_g):
    offsets = [int(o) for o in np.cumsum(IN_SPLIT_SIZES)[:-1]]
    for l in range(DEPTH):
        h = rms_norm(x, ln1_g[l])
        proj = h @ w_in[l]
        gq, gk, gv, gz, ga, gb, cq, ckv, kidx, widx = jnp.split(proj, offsets, axis=-1)
        o_a = gdn_mixer(gq, gk, gv, gz, ga, gb, conv_w[l], a_log[l], dt_bias[l], gdn_norm_g[l])
        o_b = dsa_mixer(cq, ckv, kidx, widx, q_norm_g[l], kv_norm_g[l], w_q_up[l], w_qidx_up[l],
                        w_kv_up[l], idx_ln_g[l], idx_ln_b[l], rel_bias)
        x = x + jnp.concatenate([o_a, o_b], axis=-1) @ w_out[l]
        x = x + peer_ffn(rms_norm(x, ln2_g[l]), peer_w_query[l], peer_sub_keys_1[l],
                         peer_sub_keys_2[l], peer_u[l], peer_v[l])
    return rms_norm(x, final_g)
```

```python
import functools
import math

import jax
import jax.numpy as jnp
from jax import lax
from jax.experimental import pallas as pl
from jax.experimental.pallas import tpu as pltpu

F32 = jnp.float32
BF16 = jnp.bfloat16
I32 = jnp.int32

D_MODEL = 1024
GDN_HEADS = 4
GDN_HEAD_DIM = 128
GDN_WIDTH = GDN_HEADS * GDN_HEAD_DIM
CONV_WIDTH = 4
GDN_CHUNK = 64
DSA_HEADS = 4
DSA_HEAD_DIM = 128
DSA_WIDTH = DSA_HEADS * DSA_HEAD_DIM
Q_RANK = 256
KV_RANK = 256
IDX_HEADS = 16
IDX_DIM = 64
IDX_TOPK_MAX = 256
REL_BUCKETS = 32
REL_MAX_DIST = 1024
PEER_HEADS = 8
PEER_KEYS = 128
PEER_KEY_DIM = 256
PEER_TOPK = 16
EPS = 1e-6

LANES = 128
SUBLANES = 8
VMEM_LIMIT = 56 * 1024 * 1024

SM_KIDX = 0
SM_WIDX = 64
SM_GA = 80
SM_GB = 84
INT_MIN = -2 ** 31


def _bdot(a, b):
    return jnp.dot(a.astype(BF16), b.astype(BF16), preferred_element_type=F32)


def _bdot_nt(a, b):
    return lax.dot_general(a.astype(BF16), b.astype(BF16), (((1,), (1,)), ((), ())),
                           preferred_element_type=F32)


def _bdot_tn(a, b):
    return lax.dot_general(a.astype(BF16), b.astype(BF16), (((0,), (0,)), ((), ())),
                           preferred_element_type=F32)


def _hdot(a, b):
    return jnp.dot(a, b, precision=lax.Precision.HIGHEST, preferred_element_type=F32)


def _rms(x, g):
    return x * lax.rsqrt(jnp.mean(x * x, axis=-1, keepdims=True) + EPS) * g


def _silu(x):
    return x * (1.0 / (1.0 + jnp.exp(-x)))


def _params(sem):
    return pltpu.CompilerParams(dimension_semantics=sem, vmem_limit_bytes=VMEM_LIMIT)


def _inproj_kernel(x_ref, g_ref, w_ref, gdn_ref, dsa_ref, small_ref):
    h = _rms(x_ref[...], g_ref[...])
    p = _bdot(h, w_ref[...])
    n_gdn = gdn_ref.shape[-1]
    n_dsa = dsa_ref.shape[-1]
    gdn_ref[...] = p[:, :n_gdn]
    dsa_ref[...] = p[:, n_gdn:n_gdn + n_dsa]
    small_ref[...] = p[:, n_gdn + n_dsa:]


def _inproj(x2, ln1_g, w_perm, tm):
    n = x2.shape[0]
    n_gdn = 4 * GDN_WIDTH
    n_dsa = Q_RANK + KV_RANK
    return pl.pallas_call(
        _inproj_kernel,
        grid=(n // tm,),
        in_specs=[pl.BlockSpec((tm, D_MODEL), lambda i: (i, 0)),
                  pl.BlockSpec((1, D_MODEL), lambda i: (0, 0)),
                  pl.BlockSpec(w_perm.shape, lambda i: (0, 0))],
        out_specs=[pl.BlockSpec((tm, n_gdn), lambda i: (i, 0)),
                   pl.BlockSpec((tm, n_dsa), lambda i: (i, 0)),
                   pl.BlockSpec((tm, LANES), lambda i: (i, 0))],
        out_shape=[jax.ShapeDtypeStruct((n, n_gdn), F32),
                   jax.ShapeDtypeStruct((n, n_dsa), F32),
                   jax.ShapeDtypeStruct((n, LANES), F32)],
        compiler_params=_params(("parallel",)),
    )(x2, ln1_g.reshape(1, D_MODEL), w_perm)


def _gdn_kernel(gdn_ref, small_ref, convw_ref, alog_ref, dtb_ref, ng_ref, o_ref,
                tail_ref, state_ref):
    tb = gdn_ref.shape[1]
    c = GDN_CHUNK
    n_chunks = tb // c
    w3 = 3 * GDN_WIDTH

    @pl.when(pl.program_id(1) == 0)
    def _():
        tail_ref[...] = jnp.zeros_like(tail_ref)
        state_ref[...] = jnp.zeros_like(state_ref)

    blk = gdn_ref[0]
    xin = blk[:, :w3]
    z = blk[:, w3:]
    tail = tail_ref[...]
    cw = convw_ref[...]
    acc = xin * cw[CONV_WIDTH - 1:CONV_WIDTH]
    rows8 = lax.broadcasted_iota(I32, (SUBLANES, w3), 0)
    for k in range(1, CONV_WIDTH):
        xk = pltpu.roll(xin, k, axis=0)
        fix = pltpu.roll(tail, k, axis=0)
        top = jnp.where(rows8 < k, fix, xk[:SUBLANES])
        xk = jnp.concatenate([top, xk[SUBLANES:]], axis=0)
        acc = acc + xk * cw[CONV_WIDTH - 1 - k:CONV_WIDTH - k]
    tail_ref[...] = xin[tb - SUBLANES:]
    qkv = _silu(acc)

    sm = small_ref[0]
    sp = sm + dtb_ref[...]
    softplus = jnp.maximum(sp, 0.0) + jnp.log(1.0 + jnp.exp(-jnp.abs(sp)))
    gl = -jnp.exp(alog_ref[...]) * softplus
    beta = 1.0 / (1.0 + jnp.exp(-sm))
    rin = lax.broadcasted_iota(I32, (tb, LANES), 0) % c
    s = 1
    while s < c:
        gl = gl + jnp.where(rin >= s, pltpu.roll(gl, s, axis=0), 0.0)
        s *= 2
    g_t = gl.T
    eg = jnp.exp(gl)

    ri = lax.broadcasted_iota(I32, (c, c), 0)
    ci = lax.broadcasted_iota(I32, (c, c), 1)
    tril = ri >= ci
    strict = ri > ci
    eye = (ri == ci).astype(F32)
    ng = ng_ref[...]
    scale = GDN_HEAD_DIM ** -0.5

    out_rows = []
    for ch in range(n_chunks):
        r0 = ch * c
        out_heads = []
        for h in range(GDN_HEADS):
            l0 = h * GDN_HEAD_DIM
            qh = qkv[r0:r0 + c, l0:l0 + GDN_HEAD_DIM]
            kh = qkv[r0:r0 + c, GDN_WIDTH + l0:GDN_WIDTH + l0 + GDN_HEAD_DIM]
            vh = qkv[r0:r0 + c, 2 * GDN_WIDTH + l0:2 * GDN_WIDTH + l0 + GDN_HEAD_DIM]
            qh = qh * lax.rsqrt(jnp.sum(qh * qh, axis=-1, keepdims=True) + EPS) * scale
            kh = kh * lax.rsqrt(jnp.sum(kh * kh, axis=-1, keepdims=True) + EPS)
            gcol = gl[r0:r0 + c, SM_GA + h:SM_GA + h + 1]
            grow = g_t[SM_GA + h:SM_GA + h + 1, r0:r0 + c]
            egcol = eg[r0:r0 + c, SM_GA + h:SM_GA + h + 1]
            bcol = beta[r0:r0 + c, SM_GB + h:SM_GB + h + 1]
            decay = jnp.exp(jnp.where(tril, gcol - grow, -jnp.inf))
            kb = kh * bcol
            vb = vh * bcol
            a_mat = jnp.where(strict, _bdot_nt(kb, kh) * decay, 0.0)
            m = -a_mat
            t_inv = eye + m
            p = 1
            while p < c // 2:
                m = _hdot(m, m)
                t_inv = t_inv + _hdot(t_inv, m)
                p *= 2
            u = _bdot(t_inv, vb)
            w = _bdot(t_inv, kb * egcol)
            attn = _bdot_nt(qh, kh) * decay
            st = state_ref[h]
            v_new = u - _bdot(w, st)
            o = _bdot(qh * egcol, st) + _bdot(attn, v_new)
            glast = gcol[c - 1:c]
            state_ref[h] = st * jnp.exp(glast) + _bdot_tn(kh * jnp.exp(glast - gcol), v_new)
            zh = z[r0:r0 + c, l0:l0 + GDN_HEAD_DIM]
            out_heads.append(_rms(o, ng) * _silu(zh))
        out_rows.append(jnp.concatenate(out_heads, axis=1))
    o_ref[0] = jnp.concatenate(out_rows, axis=0)


def _gdn(gdn3, small3, conv_w, a_log, dt_bias, norm_g, tb):
    b, t, _ = gdn3.shape
    alog_pad = jnp.zeros((1, LANES), F32).at[0, SM_GA:SM_GA + GDN_HEADS].set(a_log)
    dtb_pad = jnp.zeros((1, LANES), F32).at[0, SM_GA:SM_GA + GDN_HEADS].set(dt_bias)
    return pl.pallas_call(
        _gdn_kernel,
        grid=(b, t // tb),
        in_specs=[pl.BlockSpec((1, tb, 4 * GDN_WIDTH), lambda i, j: (i, j, 0)),
                  pl.BlockSpec((1, tb, LANES), lambda i, j: (i, j, 0)),
                  pl.BlockSpec((CONV_WIDTH, 3 * GDN_WIDTH), lambda i, j: (0, 0)),
                  pl.BlockSpec((1, LANES), lambda i, j: (0, 0)),
                  pl.BlockSpec((1, LANES), lambda i, j: (0, 0)),
                  pl.BlockSpec((1, GDN_HEAD_DIM), lambda i, j: (0, 0))],
        out_specs=pl.BlockSpec((1, tb, GDN_WIDTH), lambda i, j: (i, j, 0)),
        out_shape=jax.ShapeDtypeStruct((b, t, GDN_WIDTH), F32),
        scratch_shapes=[pltpu.VMEM((SUBLANES, 3 * GDN_WIDTH), F32),
                        pltpu.VMEM((GDN_HEADS, GDN_HEAD_DIM, GDN_HEAD_DIM), F32)],
        compiler_params=_params(("parallel", "arbitrary")),
    )(gdn3, small3, conv_w, alog_pad, dtb_pad, norm_g.reshape(1, GDN_HEAD_DIM))


def _dsa_prep_kernel(dsa_ref, small_ref, qg_ref, kvg_ref, wq_ref, wqi_t_ref, wk_ref, wv_t_ref,
                     lng_ref, lnb_ref, q_ref, qit_ref, k_ref, vt_ref, ki_ref, wt_ref):
    tm = dsa_ref.shape[1]
    blk = dsa_ref[0]
    cq = _rms(blk[:, :Q_RANK], qg_ref[...])
    ckv = _rms(blk[:, Q_RANK:], kvg_ref[...])
    q_ref[0] = _bdot(cq, wq_ref[...]).astype(BF16)
    k_ref[0] = _bdot(ckv, wk_ref[...]).astype(BF16)
    vt_ref[0] = _bdot_nt(wv_t_ref[...], ckv).astype(BF16)
    qit = _bdot_nt(wqi_t_ref[...], cq)
    nqb = tm // LANES
    cols = []
    for j in range(nqb):
        for h in range(IDX_HEADS):
            cols.append(qit[h * IDX_DIM:(h + 1) * IDX_DIM, j * LANES:(j + 1) * LANES])
    qit_ref[0] = jnp.concatenate(cols, axis=1).astype(BF16)
    sm = small_ref[0]
    kx = sm[:, SM_KIDX:SM_KIDX + IDX_DIM]
    mu = jnp.mean(kx, axis=-1, keepdims=True)
    xc = kx - mu
    kn = xc * lax.rsqrt(jnp.mean(xc * xc, axis=-1, keepdims=True) + EPS)
    ki_ref[0] = (kn * lng_ref[...] + lnb_ref[...]).astype(BF16)
    sm_t = sm.T
    wt_ref[0] = sm_t[SM_WIDX:SM_WIDX + IDX_HEADS] * (IDX_HEADS ** -0.5 * IDX_DIM ** -0.5)


def _dsa_prep(dsa3, small3, q_norm_g, kv_norm_g, w_q_up, w_qidx_up, w_kv_up, idx_ln_g, idx_ln_b, tm):
    b, t, _ = dsa3.shape
    wq = w_q_up.astype(BF16)
    wqi_t = w_qidx_up.T.astype(BF16)
    wkv = w_kv_up.reshape(KV_RANK, DSA_HEADS, 2, DSA_HEAD_DIM)
    wk = wkv[:, :, 0].reshape(KV_RANK, DSA_WIDTH).astype(BF16)
    wv_t = wkv[:, :, 1].reshape(KV_RANK, DSA_WIDTH).T.astype(BF16)
    full = lambda a: pl.BlockSpec(a.shape, lambda i, j: (0,) * a.ndim)
    qg = q_norm_g.reshape(1, Q_RANK)
    kvg = kv_norm_g.reshape(1, KV_RANK)
    lng = idx_ln_g.reshape(1, IDX_DIM)
    lnb = idx_ln_b.reshape(1, IDX_DIM)
    nqw = IDX_HEADS * LANES
    return pl.pallas_call(
        _dsa_prep_kernel,
        grid=(b, t // tm),
        in_specs=[pl.BlockSpec((1, tm, Q_RANK + KV_RANK), lambda i, j: (i, j, 0)),
                  pl.BlockSpec((1, tm, LANES), lambda i, j: (i, j, 0)),
                  full(qg), full(kvg), full(wq), full(wqi_t), full(wk), full(wv_t),
                  full(lng), full(lnb)],
        out_specs=[pl.BlockSpec((1, tm, DSA_WIDTH), lambda i, j: (i, j, 0)),
                   pl.BlockSpec((1, IDX_DIM, (tm // LANES) * nqw), lambda i, j: (i, 0, j)),
                   pl.BlockSpec((1, tm, DSA_WIDTH), lambda i, j: (i, j, 0)),
                   pl.BlockSpec((1, DSA_WIDTH, tm), lambda i, j: (i, 0, j)),
                   pl.BlockSpec((1, tm, IDX_DIM), lambda i, j: (i, j, 0)),
                   pl.BlockSpec((1, IDX_HEADS, tm), lambda i, j: (i, 0, j))],
        out_shape=[jax.ShapeDtypeStruct((b, t, DSA_WIDTH), BF16),
                   jax.ShapeDtypeStruct((b, IDX_DIM, (t // LANES) * nqw), BF16),
                   jax.ShapeDtypeStruct((b, t, DSA_WIDTH), BF16),
                   jax.ShapeDtypeStruct((b, DSA_WIDTH, t), BF16),
                   jax.ShapeDtypeStruct((b, t, IDX_DIM), BF16),
                   jax.ShapeDtypeStruct((b, IDX_HEADS, t), F32)],
        compiler_params=_params(("parallel", "parallel")),
    )(dsa3, small3, qg, kvg, wq, wqi_t, wk, wv_t, lng, lnb)


DSA_QB = 128
DSA_KC = 512
DSA_AC = 128
N_BIAS_TILES = 9


def _dsa_kernel(q_ref, qit_ref, wt_ref, ki_ref, k_ref, vt_ref, bias_ref, o_ref,
                keys_ref, m_ref, l_ref, acc_ref, *, topk):
    i = pl.program_id(1)
    t0 = i * DSA_QB
    n_kc = (t0 + DSA_QB + DSA_KC - 1) // DSA_KC
    tq = t0 + lax.broadcasted_iota(I32, (1, DSA_QB), 1)
    qit = qit_ref[0]
    wt = wt_ref[0]

    def score_chunk(kc, carry):
        r0 = pl.multiple_of(kc * DSA_KC, DSA_KC)
        big = jnp.dot(ki_ref[0, pl.ds(r0, DSA_KC), :], qit, preferred_element_type=F32)
        sc = jnp.zeros((DSA_KC, DSA_QB), F32)
        for h in range(IDX_HEADS):
            sc = sc + jnp.maximum(big[:, h * DSA_QB:(h + 1) * DSA_QB], 0.0) * wt[h:h + 1]
        bits = pltpu.bitcast(sc, I32)
        key = bits ^ ((bits >> 31) & 0x7FFFFFFF)
        spos = r0 + lax.broadcasted_iota(I32, (DSA_KC, 1), 0)
        keys_ref[pl.ds(r0, DSA_KC), :] = jnp.where(spos <= tq, key, INT_MIN)
        return carry

    lax.fori_loop(0, n_kc, score_chunk, 0)

    def count_ge(cand):
        def body(kc, cnt):
            r0 = pl.multiple_of(kc * DSA_KC, DSA_KC)
            ge = (keys_ref[pl.ds(r0, DSA_KC), :] >= cand).astype(I32)
            return cnt + jnp.sum(ge.reshape(DSA_KC // SUBLANES, SUBLANES, DSA_QB), axis=0)
        cnt8 = lax.fori_loop(0, n_kc, body, jnp.zeros((SUBLANES, DSA_QB), I32))
        return jnp.sum(cnt8, axis=0, keepdims=True)

    zero = jnp.zeros((1, DSA_QB), I32)
    ans = jnp.where(count_ge(zero) >= topk, zero, jnp.full((1, DSA_QB), INT_MIN, I32))

    def bit_pass(b, ans):
        cand = ans | (jnp.int32(1) << (30 - b))
        return jnp.where(count_ge(cand) >= topk, cand, ans)

    ans = lax.fori_loop(0, 31, bit_pass, ans)
    thr = jnp.maximum(ans, INT_MIN + 1)

    m_ref[...] = jnp.full_like(m_ref, -jnp.inf)
    l_ref[...] = jnp.zeros_like(l_ref)
    acc_ref[...] = jnp.zeros_like(acc_ref)
    qb = q_ref[0]
    scale = DSA_HEAD_DIM ** -0.5

    def attend(kc, carry):
        r0 = pl.multiple_of(kc * DSA_AC, DSA_AC)
        sel = keys_ref[pl.ds(r0, DSA_AC), :] >= thr
        bt = jnp.minimum(i - kc, N_BIAS_TILES - 1)
        for h in range(DSA_HEADS):
            l0 = h * DSA_HEAD_DIM
            s_t = lax.dot_general(k_ref[0, pl.ds(r0, DSA_AC), l0:l0 + DSA_HEAD_DIM],
                                  qb[:, l0:l0 + DSA_HEAD_DIM], (((1,), (1,)), ((), ())),
                                  preferred_element_type=F32)
            logit = jnp.where(sel, s_t * scale + bias_ref[h, bt], -jnp.inf)
            m_old = m_ref[h:h + 1]
            m_new = jnp.maximum(m_old, jnp.max(logit, axis=0, keepdims=True))
            m_safe = jnp.where(m_new == -jnp.inf, 0.0, m_new)
            alpha = jnp.exp(m_old - m_safe)
            p = jnp.exp(logit - m_safe)
            l_ref[h:h + 1] = alpha * l_ref[h:h + 1] + jnp.sum(p, axis=0, keepdims=True)
            pv = jnp.dot(vt_ref[0, l0:l0 + DSA_HEAD_DIM, pl.ds(r0, DSA_AC)], p.astype(BF16),
                         preferred_element_type=F32)
            acc_ref[h] = acc_ref[h] * alpha + pv
            m_ref[h:h + 1] = m_new
        return carry

    lax.fori_loop(0, i + 1, attend, 0)
    outs = []
    for h in range(DSA_HEADS):
        outs.append((acc_ref[h] / l_ref[h:h + 1]).T)
    o_ref[0] = jnp.concatenate(outs, axis=1)


def _rel_bucket(dist):
    max_exact = REL_BUCKETS // 2
    n = jnp.maximum(dist, 0)
    nf = jnp.maximum(n, 1).astype(F32)
    large = max_exact + (jnp.log(nf / max_exact) / math.log(REL_MAX_DIST / max_exact)
                         * (REL_BUCKETS - max_exact)).astype(jnp.int32)
    large = jnp.minimum(large, REL_BUCKETS - 1)
    return jnp.where(n < max_exact, n, large)


def _bias_tiles(rel_bias):
    j = jnp.arange(N_BIAS_TILES)[:, None, None]
    s = jnp.arange(DSA_AC)[None, :, None]
    t = jnp.arange(DSA_QB)[None, None, :]
    dist = j * DSA_QB + t - s
    tiles = rel_bias[_rel_bucket(dist)].astype(F32)
    return tiles.transpose(3, 0, 1, 2)


def _dsa(q, qit, wt, ki, k, vt, bias, topk):
    b, t, _ = q.shape
    nqw = IDX_HEADS * DSA_QB
    t_pad = ((t + DSA_KC - 1) // DSA_KC) * DSA_KC
    kern = functools.partial(_dsa_kernel, topk=topk)
    return pl.pallas_call(
        kern,
        grid=(b, t // DSA_QB),
        in_specs=[pl.BlockSpec((1, DSA_QB, DSA_WIDTH), lambda i, j: (i, j, 0)),
                  pl.BlockSpec((1, IDX_DIM, nqw), lambda i, j: (i, 0, j)),
                  pl.BlockSpec((1, IDX_HEADS, DSA_QB), lambda i, j: (i, 0, j)),
                  pl.BlockSpec((1, t, IDX_DIM), lambda i, j: (i, 0, 0)),
                  pl.BlockSpec((1, t, DSA_WIDTH), lambda i, j: (i, 0, 0)),
                  pl.BlockSpec((1, DSA_WIDTH, t), lambda i, j: (i, 0, 0)),
                  pl.BlockSpec(bias.shape, lambda i, j: (0, 0, 0, 0))],
        out_specs=pl.BlockSpec((1, DSA_QB, DSA_WIDTH), lambda i, j: (i, j, 0)),
        out_shape=jax.ShapeDtypeStruct((b, t, DSA_WIDTH), F32),
        scratch_shapes=[pltpu.VMEM((t_pad, DSA_QB), I32),
                        pltpu.VMEM((SUBLANES, DSA_QB), F32),
                        pltpu.VMEM((SUBLANES, DSA_QB), F32),
                        pltpu.VMEM((DSA_HEADS, DSA_HEAD_DIM, DSA_QB), F32)],
        compiler_params=_params(("parallel", "arbitrary")),
    )(q, qit, wt, ki, k, vt, bias)


def _top16_rows(s, n_rows):
    tm = s.shape[1]
    rid = lax.broadcasted_iota(I32, (n_rows, tm), 0)
    vals, idxs = [], []
    for _ in range(PEER_TOPK):
        m = jnp.max(s, axis=0, keepdims=True)
        ix = jnp.min(jnp.where(s == m, rid, n_rows), axis=0, keepdims=True)
        vals.append(m)
        idxs.append(ix)
        s = jnp.where(rid == ix, -jnp.inf, s)
    return jnp.concatenate(vals, axis=0), jnp.concatenate(idxs, axis=0)


def _mix_kernel(x_ref, oa_ref, ob_ref, wo_ref, g2_ref, wq_ref, sk1_ref, sk2_ref,
                x1_ref, h2_ref, eidx_ref, gates_ref):
    tm = x_ref.shape[0]
    o = jnp.concatenate([oa_ref[...], ob_ref[...]], axis=1)
    x1 = x_ref[...] + _bdot(o, wo_ref[...])
    x1_ref[...] = x1
    h2 = _rms(x1, g2_ref[...])
    h2_ref[...] = h2
    query = _bdot(h2, wq_ref[...])
    half = PEER_KEY_DIM // 2
    nk2 = PEER_TOPK * PEER_TOPK
    pid = lax.broadcasted_iota(I32, (nk2, tm), 0)
    e_rows, g_rows = [], []
    for h in range(PEER_HEADS):
        q1 = query[:, h * PEER_KEY_DIM:h * PEER_KEY_DIM + half]
        q2 = query[:, h * PEER_KEY_DIM + half:(h + 1) * PEER_KEY_DIM]
        s1 = _bdot_nt(sk1_ref[h], q1)
        s2 = _bdot_nt(sk2_ref[h], q2)
        v1, i1 = _top16_rows(s1, PEER_KEYS)
        v2, i2 = _top16_rows(s2, PEER_KEYS)
        cand = jnp.concatenate([v1[a:a + 1] + v2 for a in range(PEER_TOPK)], axis=0)
        cidx = jnp.concatenate([i1[a:a + 1] * PEER_KEYS + i2 for a in range(PEER_TOPK)], axis=0)
        tops, eids = [], []
        for _ in range(PEER_TOPK):
            m = jnp.max(cand, axis=0, keepdims=True)
            pos = jnp.min(jnp.where(cand == m, pid, nk2), axis=0, keepdims=True)
            hit = pid == pos
            eids.append(jnp.max(jnp.where(hit, cidx, -1), axis=0, keepdims=True))
            tops.append(m)
            cand = jnp.where(hit, -jnp.inf, cand)
        top_s = jnp.concatenate(tops, axis=0)
        ex = jnp.exp(top_s - top_s[0:1])
        g_rows.append(ex / jnp.sum(ex, axis=0, keepdims=True))
        e_rows.append(jnp.concatenate(eids, axis=0))
    gates_ref[...] = jnp.concatenate(g_rows, axis=0)
    eidx_ref[...] = jnp.concatenate(e_rows, axis=0).T


def _mix(x2, o_a, o_b, w_out, ln2_g, w_query, sk1, sk2, tm):
    n = x2.shape[0]
    wo = w_out.astype(BF16)
    wq = w_query.astype(BF16)
    sk1 = sk1.astype(BF16)
    sk2 = sk2.astype(BF16)
    nsel = PEER_HEADS * PEER_TOPK
    row = lambda w: pl.BlockSpec((tm, w), lambda i: (i, 0))
    full = lambda a: pl.BlockSpec(a.shape, lambda i: (0,) * a.ndim)
    g2 = ln2_g.reshape(1, D_MODEL)
    return pl.pallas_call(
        _mix_kernel,
        grid=(n // tm,),
        in_specs=[row(D_MODEL), row(GDN_WIDTH), row(DSA_WIDTH), full(wo), full(g2), full(wq),
                  full(sk1), full(sk2)],
        out_specs=[row(D_MODEL), row(D_MODEL), row(nsel),
                   pl.BlockSpec((nsel, tm), lambda i: (0, i))],
        out_shape=[jax.ShapeDtypeStruct((n, D_MODEL), F32),
                   jax.ShapeDtypeStruct((n, D_MODEL), F32),
                   jax.ShapeDtypeStruct((n, nsel), I32),
                   jax.ShapeDtypeStruct((nsel, n), F32)],
        compiler_params=_params(("parallel",)),
    )(x2, o_a, o_b, wo, g2, wq, sk1, sk2)


PEER_TB = 128
PEER_TG = 8


def _peer_kernel(eidx_ref, x1_ref, h2_ref, gates_ref, fg_ref, u_hbm, v_hbm, o_ref,
                 ubuf, vbuf, y_ref, sem):
    nsel = PEER_HEADS * PEER_TOPK
    n_rows = PEER_TG * nsel
    gates = gates_ref[...]
    lane = lax.broadcasted_iota(I32, (nsel, PEER_TB), 1)

    def row_copies(tok0, r):
        e = eidx_ref[tok0 + r // nsel, r % nsel]
        cu = pltpu.make_async_copy(u_hbm.at[pl.ds(e, 1)], ubuf.at[pl.ds(r, 1)], sem.at[0])
        cv = pltpu.make_async_copy(v_hbm.at[pl.ds(e, 1)], vbuf.at[pl.ds(r, 1)], sem.at[1])
        return cu, cv

    def group(g, carry):
        tok0 = g * PEER_TG

        def issue(r, c):
            cu, cv = row_copies(tok0, r)
            cu.start()
            cv.start()
            return c

        lax.fori_loop(0, n_rows, issue, 0)

        def drain(r, c):
            cu, cv = row_copies(tok0, r)
            cu.wait()
            cv.wait()
            return c

        lax.fori_loop(0, n_rows, drain, 0)

        for t in range(PEER_TG):
            tok = tok0 + t
            u = ubuf[t * nsel:(t + 1) * nsel, :]
            xt = h2_ref[pl.ds(tok, 1), :]
            act = jnp.sum(u * xt, axis=-1, keepdims=True)
            gelu = 0.5 * act * (1.0 + lax.erf(act * (2.0 ** -0.5)))
            gate = jnp.sum(jnp.where(lane == tok, gates, 0.0), axis=-1, keepdims=True)
            cf = gate * gelu
            y_ref[pl.ds(tok, 1), :] = jnp.sum(vbuf[t * nsel:(t + 1) * nsel, :] * cf, axis=0,
                                              keepdims=True)
        return carry

    lax.fori_loop(0, PEER_TB // PEER_TG, group, 0)
    o_ref[...] = _rms(x1_ref[...] + y_ref[...], fg_ref[...])


def _peer(eidx, x1, h2, gates_t, final_g, peer_u, peer_v):
    n = x1.shape[0]
    nsel = PEER_HEADS * PEER_TOPK
    fg = final_g.reshape(1, D_MODEL)
    return pl.pallas_call(
        _peer_kernel,
        grid=(n // PEER_TB,),
        in_specs=[pl.BlockSpec((PEER_TB, nsel), lambda i: (i, 0), memory_space=pltpu.SMEM),
                  pl.BlockSpec((PEER_TB, D_MODEL), lambda i: (i, 0)),
                  pl.BlockSpec((PEER_TB, D_MODEL), lambda i: (i, 0)),
                  pl.BlockSpec((nsel, PEER_TB), lambda i: (0, i)),
                  pl.BlockSpec((1, D_MODEL), lambda i: (0, 0)),
                  pl.BlockSpec(memory_space=pl.ANY),
                  pl.BlockSpec(memory_space=pl.ANY)],
        out_specs=pl.BlockSpec((PEER_TB, D_MODEL), lambda i: (i, 0)),
        out_shape=jax.ShapeDtypeStruct((n, D_MODEL), F32),
        scratch_shapes=[pltpu.VMEM((PEER_TG * nsel, D_MODEL), F32),
                        pltpu.VMEM((PEER_TG * nsel, D_MODEL), F32),
                        pltpu.VMEM((PEER_TB, D_MODEL), F32),
                        pltpu.SemaphoreType.DMA((2,))],
        compiler_params=_params(("arbitrary",)),
    )(eidx, x1, h2, gates_t, fg, peer_u, peer_v)


def _permute_w_in(w_in):
    o = [0]
    for s in (GDN_WIDTH, GDN_WIDTH, GDN_WIDTH, GDN_WIDTH, GDN_HEADS, GDN_HEADS, Q_RANK, KV_RANK,
              IDX_DIM, IDX_HEADS):
        o.append(o[-1] + s)
    gq_gz = w_in[:, o[0]:o[4]]
    ga = w_in[:, o[4]:o[5]]
    gb = w_in[:, o[5]:o[6]]
    cq_ckv = w_in[:, o[6]:o[8]]
    kidx = w_in[:, o[8]:o[9]]
    widx = w_in[:, o[9]:o[10]]
    pad = jnp.zeros((w_in.shape[0], LANES - (IDX_DIM + IDX_HEADS + 2 * GDN_HEADS)), w_in.dtype)
    return jnp.concatenate([gq_gz, cq_ckv, kidx, widx, ga, gb, pad], axis=1).astype(BF16)


def kernel(x, ln1_g, w_in, conv_w, a_log, dt_bias, gdn_norm_g, q_norm_g, kv_norm_g, w_q_up,
           w_qidx_up, w_kv_up, idx_ln_g, idx_ln_b, w_out, ln2_g, peer_w_query, peer_sub_keys_1,
           peer_sub_keys_2, peer_u, peer_v, rel_bias, final_g):
    b, t, d = x.shape
    n = b * t
    assert w_in.shape[0] == 1, "single-layer block only"
    l = 0
    topk = min(IDX_TOPK_MAX, t // 4)
    bias = _bias_tiles(rel_bias)
    xc = x.reshape(n, d)
    gdn, dsa, small = _inproj(xc, ln1_g[l], _permute_w_in(w_in[l]), tm=256)
    gdn3 = gdn.reshape(b, t, -1)
    dsa3 = dsa.reshape(b, t, -1)
    small3 = small.reshape(b, t, LANES)
    o_a = _gdn(gdn3, small3, conv_w[l], a_log[l], dt_bias[l], gdn_norm_g[l], tb=256)
    q, qit, k, vt, ki, wt = _dsa_prep(dsa3, small3, q_norm_g[l], kv_norm_g[l], w_q_up[l],
                                      w_qidx_up[l], w_kv_up[l], idx_ln_g[l], idx_ln_b[l], tm=512)
    o_b = _dsa(q, qit, wt, ki, k, vt, bias, topk)
    x1, h2, eidx, gates_t = _mix(xc, o_a.reshape(n, -1), o_b.reshape(n, -1), w_out[l], ln2_g[l],
                                 peer_w_query[l], peer_sub_keys_1[l], peer_sub_keys_2[l], tm=256)
    out = _peer(eidx, x1, h2, gates_t, final_g, peer_u[l], peer_v[l])
    return out.reshape(b, t, d)
```

```python
import functools
import math

import jax
import jax.numpy as jnp
from jax import lax
from jax.experimental import pallas as pl
from jax.experimental.pallas import tpu as pltpu

F32 = jnp.float32
BF16 = jnp.bfloat16
I32 = jnp.int32

D_MODEL = 1024
GDN_HEADS = 4
GDN_HEAD_DIM = 128
GDN_WIDTH = GDN_HEADS * GDN_HEAD_DIM
CONV_WIDTH = 4
GDN_CHUNK = 64
DSA_HEADS = 4
DSA_HEAD_DIM = 128
DSA_WIDTH = DSA_HEADS * DSA_HEAD_DIM
Q_RANK = 256
KV_RANK = 256
IDX_HEADS = 16
IDX_DIM = 64
IDX_TOPK_MAX = 256
REL_BUCKETS = 32
REL_MAX_DIST = 1024
PEER_HEADS = 8
PEER_KEYS = 128
PEER_KEY_DIM = 256
PEER_TOPK = 16
EPS = 1e-6

LANES = 128
SUBLANES = 8
VMEM_LIMIT = 56 * 1024 * 1024

SM_KIDX = 0
SM_WIDX = 64
SM_GA = 80
SM_GB = 84
INT_MIN = -2 ** 31


def _bdot(a, b):
    return jnp.dot(a.astype(BF16), b.astype(BF16), preferred_element_type=F32)


def _bdot_nt(a, b):
    return lax.dot_general(a.astype(BF16), b.astype(BF16), (((1,), (1,)), ((), ())),
                           preferred_element_type=F32)


def _bdot_tn(a, b):
    return lax.dot_general(a.astype(BF16), b.astype(BF16), (((0,), (0,)), ((), ())),
                           preferred_element_type=F32)


def _hdot(a, b):
    return jnp.dot(a, b, precision=lax.Precision.HIGHEST, preferred_element_type=F32)


def _rms(x, g):
    return x * lax.rsqrt(jnp.mean(x * x, axis=-1, keepdims=True) + EPS) * g


def _silu(x):
    return x * (1.0 / (1.0 + jnp.exp(-x)))


def _params(sem):
    return pltpu.CompilerParams(dimension_semantics=sem, vmem_limit_bytes=VMEM_LIMIT)


def _inproj_kernel(x_ref, g_ref, w_ref, gdn_ref, dsa_ref, small_ref):
    h = _rms(x_ref[...], g_ref[...])
    p = _bdot(h, w_ref[...])
    n_gdn = gdn_ref.shape[-1]
    n_dsa = dsa_ref.shape[-1]
    gdn_ref[...] = p[:, :n_gdn]
    dsa_ref[...] = p[:, n_gdn:n_gdn + n_dsa]
    small_ref[...] = p[:, n_gdn + n_dsa:]


def _inproj(x2, ln1_g, w_perm, tm):
    n = x2.shape[0]
    n_gdn = 4 * GDN_WIDTH
    n_dsa = Q_RANK + KV_RANK
    return pl.pallas_call(
        _inproj_kernel,
        grid=(n // tm,),
        in_specs=[pl.BlockSpec((tm, D_MODEL), lambda i: (i, 0)),
                  pl.BlockSpec((1, D_MODEL), lambda i: (0, 0)),
                  pl.BlockSpec(w_perm.shape, lambda i: (0, 0))],
        out_specs=[pl.BlockSpec((tm, n_gdn), lambda i: (i, 0)),
                   pl.BlockSpec((tm, n_dsa), lambda i: (i, 0)),
                   pl.BlockSpec((tm, LANES), lambda i: (i, 0))],
        out_shape=[jax.ShapeDtypeStruct((n, n_gdn), F32),
                   jax.ShapeDtypeStruct((n, n_dsa), F32),
                   jax.ShapeDtypeStruct((n, LANES), F32)],
        compiler_params=_params(("parallel",)),
    )(x2, ln1_g.reshape(1, D_MODEL), w_perm)


def _gdn_kernel(gdn_ref, small_ref, convw_ref, alog_ref, dtb_ref, ng_ref, o_ref,
                tail_ref, state_ref):
    tb = gdn_ref.shape[1]
    c = GDN_CHUNK
    n_chunks = tb // c
    w3 = 3 * GDN_WIDTH

    @pl.when(pl.program_id(1) == 0)
    def _():
        tail_ref[...] = jnp.zeros_like(tail_ref)
        state_ref[...] = jnp.zeros_like(state_ref)

    blk = gdn_ref[0]
    xin = blk[:, :w3]
    z = blk[:, w3:]
    tail = tail_ref[...]
    cw = convw_ref[...]
    acc = xin * cw[CONV_WIDTH - 1:CONV_WIDTH]
    rows8 = lax.broadcasted_iota(I32, (SUBLANES, w3), 0)
    for k in range(1, CONV_WIDTH):
        xk = pltpu.roll(xin, k, axis=0)
        fix = pltpu.roll(tail, k, axis=0)
        top = jnp.where(rows8 < k, fix, xk[:SUBLANES])
        xk = jnp.concatenate([top, xk[SUBLANES:]], axis=0)
        acc = acc + xk * cw[CONV_WIDTH - 1 - k:CONV_WIDTH - k]
    tail_ref[...] = xin[tb - SUBLANES:]
    qkv = _silu(acc)

    sm = small_ref[0]
    sp = sm + dtb_ref[...]
    softplus = jnp.maximum(sp, 0.0) + jnp.log(1.0 + jnp.exp(-jnp.abs(sp)))
    gl = -jnp.exp(alog_ref[...]) * softplus
    beta = 1.0 / (1.0 + jnp.exp(-sm))
    rin = lax.broadcasted_iota(I32, (tb, LANES), 0) % c
    s = 1
    while s < c:
        gl = gl + jnp.where(rin >= s, pltpu.roll(gl, s, axis=0), 0.0)
        s *= 2
    g_t = gl.T
    eg = jnp.exp(gl)

    ri = lax.broadcasted_iota(I32, (c, c), 0)
    ci = lax.broadcasted_iota(I32, (c, c), 1)
    tril = ri >= ci
    strict = ri > ci
    eye = (ri == ci).astype(F32)
    ng = ng_ref[...]
    scale = GDN_HEAD_DIM ** -0.5

    out_rows = []
    for ch in range(n_chunks):
        r0 = ch * c
        out_heads = []
        for h in range(GDN_HEADS):
            l0 = h * GDN_HEAD_DIM
            qh = qkv[r0:r0 + c, l0:l0 + GDN_HEAD_DIM]
            kh = qkv[r0:r0 + c, GDN_WIDTH + l0:GDN_WIDTH + l0 + GDN_HEAD_DIM]
            vh = qkv[r0:r0 + c, 2 * GDN_WIDTH + l0:2 * GDN_WIDTH + l0 + GDN_HEAD_DIM]
            qh = qh * lax.rsqrt(jnp.sum(qh * qh, axis=-1, keepdims=True) + EPS) * scale
            kh = kh * lax.rsqrt(jnp.sum(kh * kh, axis=-1, keepdims=True) + EPS)
            gcol = gl[r0:r0 + c, SM_GA + h:SM_GA + h + 1]
            grow = g_t[SM_GA + h:SM_GA + h + 1, r0:r0 + c]
            egcol = eg[r0:r0 + c, SM_GA + h:SM_GA + h + 1]
            bcol = beta[r0:r0 + c, SM_GB + h:SM_GB + h + 1]
            decay = jnp.exp(jnp.where(tril, gcol - grow, -jnp.inf))
            kb = kh * bcol
            vb = vh * bcol
            a_mat = jnp.where(strict, _bdot_nt(kb, kh) * decay, 0.0)
            m = -a_mat
            t_inv = eye + m
            p = 1
            while p < c // 2:
                m = _hdot(m, m)
                t_inv = t_inv + _hdot(t_inv, m)
                p *= 2
            u = _bdot(t_inv, vb)
            w = _bdot(t_inv, kb * egcol)
            attn = _bdot_nt(qh, kh) * decay
            st = state_ref[h]
            v_new = u - _bdot(w, st)
            o = _bdot(qh * egcol, st) + _bdot(attn, v_new)
            glast = gcol[c - 1:c]
            state_ref[h] = st * jnp.exp(glast) + _bdot_tn(kh * jnp.exp(glast - gcol), v_new)
            zh = z[r0:r0 + c, l0:l0 + GDN_HEAD_DIM]
            out_heads.append(_rms(o, ng) * _silu(zh))
        out_rows.append(jnp.concatenate(out_heads, axis=1))
    o_ref[0] = jnp.concatenate(out_rows, axis=0)


def _gdn(gdn3, small3, conv_w, a_log, dt_bias, norm_g, tb):
    b, t, _ = gdn3.shape
    alog_pad = jnp.zeros((1, LANES), F32).at[0, SM_GA:SM_GA + GDN_HEADS].set(a_log)
    dtb_pad = jnp.zeros((1, LANES), F32).at[0, SM_GA:SM_GA + GDN_HEADS].set(dt_bias)
    return pl.pallas_call(
        _gdn_kernel,
        grid=(b, t // tb),
        in_specs=[pl.BlockSpec((1, tb, 4 * GDN_WIDTH), lambda i, j: (i, j, 0)),
                  pl.BlockSpec((1, tb, LANES), lambda i, j: (i, j, 0)),
                  pl.BlockSpec((CONV_WIDTH, 3 * GDN_WIDTH), lambda i, j: (0, 0)),
                  pl.BlockSpec((1, LANES), lambda i, j: (0, 0)),
                  pl.BlockSpec((1, LANES), lambda i, j: (0, 0)),
                  pl.BlockSpec((1, GDN_HEAD_DIM), lambda i, j: (0, 0))],
        out_specs=pl.BlockSpec((1, tb, GDN_WIDTH), lambda i, j: (i, j, 0)),
        out_shape=jax.ShapeDtypeStruct((b, t, GDN_WIDTH), F32),
        scratch_shapes=[pltpu.VMEM((SUBLANES, 3 * GDN_WIDTH), F32),
                        pltpu.VMEM((GDN_HEADS, GDN_HEAD_DIM, GDN_HEAD_DIM), F32)],
        compiler_params=_params(("parallel", "arbitrary")),
    )(gdn3, small3, conv_w, alog_pad, dtb_pad, norm_g.reshape(1, GDN_HEAD_DIM))


def _dsa_prep_kernel(dsa_ref, small_ref, qg_ref, kvg_ref, wq_ref, wqi_t_ref, wk_ref, wv_t_ref,
                     lng_ref, lnb_ref, q_ref, qit_ref, k_ref, vt_ref, ki_ref, wt_ref):
    tm = dsa_ref.shape[1]
    blk = dsa_ref[0]
    cq = _rms(blk[:, :Q_RANK], qg_ref[...])
    ckv = _rms(blk[:, Q_RANK:], kvg_ref[...])
    q_ref[0] = _bdot(cq, wq_ref[...]).astype(BF16)
    k_ref[0] = _bdot(ckv, wk_ref[...]).astype(BF16)
    vt_ref[0] = _bdot_nt(wv_t_ref[...], ckv).astype(BF16)
    qit = _bdot_nt(wqi_t_ref[...], cq)
    nqb = tm // LANES
    cols = []
    for j in range(nqb):
        for h in range(IDX_HEADS):
            cols.append(qit[h * IDX_DIM:(h + 1) * IDX_DIM, j * LANES:(j + 1) * LANES])
    qit_ref[0] = jnp.concatenate(cols, axis=1).astype(BF16)
    sm = small_ref[0]
    kx = sm[:, SM_KIDX:SM_KIDX + IDX_DIM]
    mu = jnp.mean(kx, axis=-1, keepdims=True)
    xc = kx - mu
    kn = xc * lax.rsqrt(jnp.mean(xc * xc, axis=-1, keepdims=True) + EPS)
    ki_ref[0] = (kn * lng_ref[...] + lnb_ref[...]).astype(BF16)
    sm_t = sm.T
    wt_ref[0] = sm_t[SM_WIDX:SM_WIDX + IDX_HEADS] * (IDX_HEADS ** -0.5 * IDX_DIM ** -0.5)


def _dsa_prep(dsa3, small3, q_norm_g, kv_norm_g, w_q_up, w_qidx_up, w_kv_up, idx_ln_g, idx_ln_b, tm):
    b, t, _ = dsa3.shape
    wq = w_q_up.astype(BF16)
    wqi_t = w_qidx_up.T.astype(BF16)
    wkv = w_kv_up.reshape(KV_RANK, DSA_HEADS, 2, DSA_HEAD_DIM)
    wk = wkv[:, :, 0].reshape(KV_RANK, DSA_WIDTH).astype(BF16)
    wv_t = wkv[:, :, 1].reshape(KV_RANK, DSA_WIDTH).T.astype(BF16)
    full = lambda a: pl.BlockSpec(a.shape, lambda i, j: (0,) * a.ndim)
    qg = q_norm_g.reshape(1, Q_RANK)
    kvg = kv_norm_g.reshape(1, KV_RANK)
    lng = idx_ln_g.reshape(1, IDX_DIM)
    lnb = idx_ln_b.reshape(1, IDX_DIM)
    nqw = IDX_HEADS * LANES
    return pl.pallas_call(
        _dsa_prep_kernel,
        grid=(b, t // tm),
        in_specs=[pl.BlockSpec((1, tm, Q_RANK + KV_RANK), lambda i, j: (i, j, 0)),
                  pl.BlockSpec((1, tm, LANES), lambda i, j: (i, j, 0)),
                  full(qg), full(kvg), full(wq), full(wqi_t), full(wk), full(wv_t),
                  full(lng), full(lnb)],
        out_specs=[pl.BlockSpec((1, tm, DSA_WIDTH), lambda i, j: (i, j, 0)),
                   pl.BlockSpec((1, IDX_DIM, (tm // LANES) * nqw), lambda i, j: (i, 0, j)),
                   pl.BlockSpec((1, tm, DSA_WIDTH), lambda i, j: (i, j, 0)),
                   pl.BlockSpec((1, DSA_WIDTH, tm), lambda i, j: (i, 0, j)),
                   pl.BlockSpec((1, tm, IDX_DIM), lambda i, j: (i, j, 0)),
                   pl.BlockSpec((1, IDX_HEADS, tm), lambda i, j: (i, 0, j))],
        out_shape=[jax.ShapeDtypeStruct((b, t, DSA_WIDTH), BF16),
                   jax.ShapeDtypeStruct((b, IDX_DIM, (t // LANES) * nqw), BF16),
                   jax.ShapeDtypeStruct((b, t, DSA_WIDTH), BF16),
                   jax.ShapeDtypeStruct((b, DSA_WIDTH, t), BF16),
                   jax.ShapeDtypeStruct((b, t, IDX_DIM), BF16),
                   jax.ShapeDtypeStruct((b, IDX_HEADS, t), F32)],
        compiler_params=_params(("parallel", "parallel")),
    )(dsa3, small3, qg, kvg, wq, wqi_t, wk, wv_t, lng, lnb)


DSA_QB = 128
DSA_KC = 512
DSA_AC = 128
N_BIAS_TILES = 9


def _dsa_kernel(q_ref, qit_ref, wt_ref, ki_ref, k_ref, vt_ref, bias_ref, o_ref,
                keys_ref, m_ref, l_ref, acc_ref, *, topk):
    i = pl.program_id(1)
    t0 = i * DSA_QB
    n_kc = (t0 + DSA_QB + DSA_KC - 1) // DSA_KC
    tq = t0 + lax.broadcasted_iota(I32, (1, DSA_QB), 1)
    qit = qit_ref[0]
    wt = wt_ref[0]

    def score_chunk(kc, carry):
        r0 = pl.multiple_of(kc * DSA_KC, DSA_KC)
        big = jnp.dot(ki_ref[0, pl.ds(r0, DSA_KC), :], qit, preferred_element_type=F32)
        sc = jnp.zeros((DSA_KC, DSA_QB), F32)
        for h in range(IDX_HEADS):
            sc = sc + jnp.maximum(big[:, h * DSA_QB:(h + 1) * DSA_QB], 0.0) * wt[h:h + 1]
        bits = pltpu.bitcast(sc, I32)
        key = bits ^ ((bits >> 31) & 0x7FFFFFFF)
        spos = r0 + lax.broadcasted_iota(I32, (DSA_KC, 1), 0)
        keys_ref[pl.ds(r0, DSA_KC), :] = jnp.where(spos <= tq, key, INT_MIN)
        return carry

    lax.fori_loop(0, n_kc, score_chunk, 0)

    def count_ge(cand):
        def body(kc, cnt):
            r0 = pl.multiple_of(kc * DSA_KC, DSA_KC)
            ge = (keys_ref[pl.ds(r0, DSA_KC), :] >= cand).astype(I32)
            return cnt + jnp.sum(ge.reshape(DSA_KC // SUBLANES, SUBLANES, DSA_QB), axis=0)
        cnt8 = lax.fori_loop(0, n_kc, body, jnp.zeros((SUBLANES, DSA_QB), I32))
        return jnp.sum(cnt8, axis=0, keepdims=True)

    zero = jnp.zeros((1, DSA_QB), I32)
    ans = jnp.where(count_ge(zero) >= topk, zero, jnp.full((1, DSA_QB), INT_MIN, I32))

    def bit_pass(b, ans):
        cand = ans | (jnp.int32(1) << (30 - b))
        return jnp.where(count_ge(cand) >= topk, cand, ans)

    ans = lax.fori_loop(0, 31, bit_pass, ans)
    thr = jnp.maximum(ans, INT_MIN + 1)

    m_ref[...] = jnp.full_like(m_ref, -jnp.inf)
    l_ref[...] = jnp.zeros_like(l_ref)
    acc_ref[...] = jnp.zeros_like(acc_ref)
    qb = q_ref[0]
    scale = DSA_HEAD_DIM ** -0.5

    def attend(kc, carry):
        r0 = pl.multiple_of(kc * DSA_AC, DSA_AC)
        sel = keys_ref[pl.ds(r0, DSA_AC), :] >= thr
        bt = jnp.minimum(i - kc, N_BIAS_TILES - 1)
        for h in range(DSA_HEADS):
            l0 = h * DSA_HEAD_DIM
            s_t = lax.dot_general(k_ref[0, pl.ds(r0, DSA_AC), l0:l0 + DSA_HEAD_DIM],
                                  qb[:, l0:l0 + DSA_HEAD_DIM], (((1,), (1,)), ((), ())),
                                  preferred_element_type=F32)
            logit = jnp.where(sel, s_t * scale + bias_ref[h, bt], -jnp.inf)
            m_old = m_ref[h:h + 1]
            m_new = jnp.maximum(m_old, jnp.max(logit, axis=0, keepdims=True))
            m_safe = jnp.where(m_new == -jnp.inf, 0.0, m_new)
            alpha = jnp.exp(m_old - m_safe)
            p = jnp.exp(logit - m_safe)
            l_ref[h:h + 1] = alpha * l_ref[h:h + 1] + jnp.sum(p, axis=0, keepdims=True)
            pv = jnp.dot(vt_ref[0, l0:l0 + DSA_HEAD_DIM, pl.ds(r0, DSA_AC)], p.astype(BF16),
                         preferred_element_type=F32)
            acc_ref[h] = acc_ref[h] * alpha + pv
            m_ref[h:h + 1] = m_new
        return carry

    lax.fori_loop(0, i + 1, attend, 0)
    outs = []
    for h in range(DSA_HEADS):
        outs.append((acc_ref[h] / l_ref[h:h + 1]).T)
    o_ref[0] = jnp.concatenate(outs, axis=1)


def _rel_bucket(dist):
    max_exact = REL_BUCKETS // 2
    n = jnp.maximum(dist, 0)
    nf = jnp.maximum(n, 1).astype(F32)
    large = max_exact + (jnp.log(nf / max_exact) / math.log(REL_MAX_DIST / max_exact)
                         * (REL_BUCKETS - max_exact)).astype(jnp.int32)
    large = jnp.minimum(large, REL_BUCKETS - 1)
    return jnp.where(n < max_exact, n, large)


def _bias_tiles(rel_bias):
    span = DSA_AC + DSA_QB
    dist = jnp.arange(-(DSA_AC - 1), N_BIAS_TILES * DSA_QB + 1)
    by_dist = rel_bias[_rel_bucket(dist)].astype(F32).T
    rows = jnp.stack([by_dist[:, j * DSA_QB:j * DSA_QB + span] for j in range(N_BIAS_TILES)], axis=1)
    rep = jnp.tile(rows, (1, 1, DSA_AC))[..., DSA_AC - 1:DSA_AC - 1 + DSA_AC * (span - 1)]
    return rep.reshape(DSA_HEADS, N_BIAS_TILES, DSA_AC, span - 1)[..., :DSA_QB]


def _dsa(q, qit, wt, ki, k, vt, bias, topk):
    b, t, _ = q.shape
    nqw = IDX_HEADS * DSA_QB
    t_pad = ((t + DSA_KC - 1) // DSA_KC) * DSA_KC
    kern = functools.partial(_dsa_kernel, topk=topk)
    return pl.pallas_call(
        kern,
        grid=(b, t // DSA_QB),
        in_specs=[pl.BlockSpec((1, DSA_QB, DSA_WIDTH), lambda i, j: (i, j, 0)),
                  pl.BlockSpec((1, IDX_DIM, nqw), lambda i, j: (i, 0, j)),
                  pl.BlockSpec((1, IDX_HEADS, DSA_QB), lambda i, j: (i, 0, j)),
                  pl.BlockSpec((1, t, IDX_DIM), lambda i, j: (i, 0, 0)),
                  pl.BlockSpec((1, t, DSA_WIDTH), lambda i, j: (i, 0, 0)),
                  pl.BlockSpec((1, DSA_WIDTH, t), lambda i, j: (i, 0, 0)),
                  pl.BlockSpec(bias.shape, lambda i, j: (0, 0, 0, 0))],
        out_specs=pl.BlockSpec((1, DSA_QB, DSA_WIDTH), lambda i, j: (i, j, 0)),
        out_shape=jax.ShapeDtypeStruct((b, t, DSA_WIDTH), F32),
        scratch_shapes=[pltpu.VMEM((t_pad, DSA_QB), I32),
                        pltpu.VMEM((SUBLANES, DSA_QB), F32),
                        pltpu.VMEM((SUBLANES, DSA_QB), F32),
                        pltpu.VMEM((DSA_HEADS, DSA_HEAD_DIM, DSA_QB), F32)],
        compiler_params=_params(("parallel", "arbitrary")),
    )(q, qit, wt, ki, k, vt, bias)


def _top16_rows(s, n_rows):
    tm = s.shape[1]
    rid = lax.broadcasted_iota(I32, (n_rows, tm), 0)
    vals, idxs = [], []
    for _ in range(PEER_TOPK):
        m = jnp.max(s, axis=0, keepdims=True)
        ix = jnp.min(jnp.where(s == m, rid, n_rows), axis=0, keepdims=True)
        vals.append(m)
        idxs.append(ix)
        s = jnp.where(rid == ix, -jnp.inf, s)
    return jnp.concatenate(vals, axis=0), jnp.concatenate(idxs, axis=0)


def _mix_kernel(x_ref, oa_ref, ob_ref, wo_ref, g2_ref, wq_ref, sk1_ref, sk2_ref,
                x1_ref, h2_ref, eidx_ref, gates_ref):
    tm = x_ref.shape[0]
    o = jnp.concatenate([oa_ref[...], ob_ref[...]], axis=1)
    x1 = x_ref[...] + _bdot(o, wo_ref[...])
    x1_ref[...] = x1
    h2 = _rms(x1, g2_ref[...])
    h2_ref[...] = h2
    query = _bdot(h2, wq_ref[...])
    half = PEER_KEY_DIM // 2
    nk2 = PEER_TOPK * PEER_TOPK
    pid = lax.broadcasted_iota(I32, (nk2, tm), 0)
    e_rows, g_rows = [], []
    for h in range(PEER_HEADS):
        q1 = query[:, h * PEER_KEY_DIM:h * PEER_KEY_DIM + half]
        q2 = query[:, h * PEER_KEY_DIM + half:(h + 1) * PEER_KEY_DIM]
        s1 = _bdot_nt(sk1_ref[h], q1)
        s2 = _bdot_nt(sk2_ref[h], q2)
        v1, i1 = _top16_rows(s1, PEER_KEYS)
        v2, i2 = _top16_rows(s2, PEER_KEYS)
        cand = jnp.concatenate([v1[a:a + 1] + v2 for a in range(PEER_TOPK)], axis=0)
        cidx = jnp.concatenate([i1[a:a + 1] * PEER_KEYS + i2 for a in range(PEER_TOPK)], axis=0)
        tops, eids = [], []
        for _ in range(PEER_TOPK):
            m = jnp.max(cand, axis=0, keepdims=True)
            pos = jnp.min(jnp.where(cand == m, pid, nk2), axis=0, keepdims=True)
            hit = pid == pos
            eids.append(jnp.max(jnp.where(hit, cidx, -1), axis=0, keepdims=True))
            tops.append(m)
            cand = jnp.where(hit, -jnp.inf, cand)
        top_s = jnp.concatenate(tops, axis=0)
        ex = jnp.exp(top_s - top_s[0:1])
        g_rows.append(ex / jnp.sum(ex, axis=0, keepdims=True))
        e_rows.append(jnp.concatenate(eids, axis=0))
    gates_ref[...] = jnp.concatenate(g_rows, axis=0).T
    eidx_ref[...] = jnp.concatenate(e_rows, axis=0).T


def _mix(x2, o_a, o_b, w_out, ln2_g, w_query, sk1, sk2, tm):
    n = x2.shape[0]
    wo = w_out.astype(BF16)
    wq = w_query.astype(BF16)
    sk1 = sk1.astype(BF16)
    sk2 = sk2.astype(BF16)
    nsel = PEER_HEADS * PEER_TOPK
    row = lambda w: pl.BlockSpec((tm, w), lambda i: (i, 0))
    full = lambda a: pl.BlockSpec(a.shape, lambda i: (0,) * a.ndim)
    g2 = ln2_g.reshape(1, D_MODEL)
    return pl.pallas_call(
        _mix_kernel,
        grid=(n // tm,),
        in_specs=[row(D_MODEL), row(GDN_WIDTH), row(DSA_WIDTH), full(wo), full(g2), full(wq),
                  full(sk1), full(sk2)],
        out_specs=[row(D_MODEL), row(D_MODEL), row(nsel), row(nsel)],
        out_shape=[jax.ShapeDtypeStruct((n, D_MODEL), F32),
                   jax.ShapeDtypeStruct((n, D_MODEL), F32),
                   jax.ShapeDtypeStruct((n, nsel), I32),
                   jax.ShapeDtypeStruct((n, nsel), F32)],
        compiler_params=_params(("parallel",)),
    )(x2, o_a, o_b, wo, g2, wq, sk1, sk2)


PEER_TB = 128
PEER_SLAB = 4
PEER_HALF = D_MODEL // 2
PEER_REP = PEER_HALF // (PEER_HEADS * PEER_TOPK)


def _pack_table(w):
    e = w.shape[0]
    bits = lax.bitcast_convert_type(w.astype(jnp.bfloat16), jnp.uint16).astype(jnp.uint32)
    words = bits[:, :PEER_HALF] | (bits[:, PEER_HALF:] << 16)
    return lax.bitcast_convert_type(words, I32).reshape(e * PEER_SLAB, LANES)


def _unpack_slabs(words):
    lo = lax.bitcast_convert_type(words << 16, F32)
    hi = lax.bitcast_convert_type(words & jnp.int32(-65536), F32)
    return lo, hi


def _split_bf16(x):
    head = x.astype(jnp.bfloat16).astype(F32)
    return head, x - head


def _gather_slabs(eidx_ref, tok, tab_ref, stage_ref):
    nsel = PEER_HEADS * PEER_TOPK
    for k in range(nsel):
        row = pl.multiple_of(eidx_ref[tok, k] * PEER_SLAB, PEER_SLAB)
        stage_ref[k * PEER_SLAB:(k + 1) * PEER_SLAB, :] = tab_ref[pl.ds(row, PEER_SLAB), :]


def _load_table(tab_hbm, tab_ref, sem):
    @pl.when(pl.program_id(0) == 0)
    def _():
        cp = pltpu.make_async_copy(tab_hbm, tab_ref, sem)
        cp.start()
        cp.wait()


def _slab_mask():
    j = lax.broadcasted_iota(I32, (SUBLANES, PEER_HALF), 0)
    col = lax.broadcasted_iota(I32, (SUBLANES, PEER_HALF), 1)
    return (col % PEER_SLAB) == (j % PEER_SLAB)


def _peer_u_kernel(eidx_ref, x3_ref, gates_ref, rep_ref, tab_hbm, c_ref, tab_ref, st_a, st_b, z_ref, sem):
    _load_table(tab_hbm, tab_ref, sem)
    mask = _slab_mask()
    lane = lax.broadcasted_iota(I32, (SUBLANES, PEER_HALF), 1)

    def token(tok, stage_ref):
        _gather_slabs(eidx_ref, tok, tab_ref, stage_ref)
        lo, hi = _unpack_slabs(stage_ref[...])
        xh, xt = _split_bf16(x3_ref[tok])
        a_lo = jnp.concatenate([xh[:PEER_SLAB], xt[:PEER_SLAB]], axis=0)
        a_hi = jnp.concatenate([xh[PEER_SLAB:], xt[PEER_SLAB:]], axis=0)
        r = _bdot_nt(a_lo, lo) + _bdot_nt(a_hi, hi)
        z_ref[pl.ds(tok, 1), :] = jnp.sum(jnp.where(mask, r, 0.0), axis=0, keepdims=True)

    def pair(i, carry):
        token(2 * i, st_a)
        token(2 * i + 1, st_b)
        return carry

    lax.fori_loop(0, PEER_TB // 2, pair, 0)

    gh, gt = _split_bf16(gates_ref[...])
    rep = rep_ref[...]
    gates_rep = _bdot(gh, rep) + _bdot(gt, rep)
    for g in range(PEER_TB // SUBLANES):
        z = z_ref[g * SUBLANES:(g + 1) * SUBLANES, :]
        s = 1
        while s < PEER_SLAB:
            up = pltpu.roll(z, PEER_HALF - s, axis=1)
            dn = pltpu.roll(z, s, axis=1)
            z = z + jnp.where((lane // s) % 2 == 0, up, dn)
            s *= 2
        gelu = 0.5 * z * (1.0 + lax.erf(z * (2.0 ** -0.5)))
        c_ref[g * SUBLANES:(g + 1) * SUBLANES, :] = gelu * gates_rep[g * SUBLANES:(g + 1) * SUBLANES]


def _peer_v_kernel(eidx_ref, c_in_ref, x1_ref, fg_ref, tab_hbm, o_ref, tab_ref, st_a, st_b, sem):
    _load_table(tab_hbm, tab_ref, sem)
    mask = _slab_mask()

    def token(tok, stage_ref):
        _gather_slabs(eidx_ref, tok, tab_ref, stage_ref)
        lo, hi = _unpack_slabs(stage_ref[...])
        ch, ct = _split_bf16(c_in_ref[pl.ds(tok, 1), :])
        rows = lax.broadcasted_iota(I32, (SUBLANES, PEER_HALF), 0)
        c8 = jnp.where(mask, jnp.where(rows < PEER_SLAB, ch, ct), 0.0)
        y_lo = _bdot(c8, lo)
        y_hi = _bdot(c8, hi)
        y = jnp.concatenate([y_lo[:PEER_SLAB] + y_lo[PEER_SLAB:], y_hi[:PEER_SLAB] + y_hi[PEER_SLAB:]],
                            axis=0)
        zt = x1_ref[tok] + y
        ms = jnp.sum(jnp.sum(zt * zt, axis=1, keepdims=True), axis=0, keepdims=True) * (1.0 / D_MODEL)
        o_ref[tok] = zt * lax.rsqrt(ms + EPS) * fg_ref[...]

    def pair(i, carry):
        token(2 * i, st_a)
        token(2 * i + 1, st_b)
        return carry

    lax.fori_loop(0, PEER_TB // 2, pair, 0)


def _peer(eidx, x1, h2, gates, final_g, peer_u, peer_v):
    n = x1.shape[0]
    nsel = PEER_HEADS * PEER_TOPK
    u_tab = _pack_table(peer_u)
    v_tab = _pack_table(peer_v)
    x3 = h2.reshape(n, SUBLANES, LANES)
    x1_3 = x1.reshape(n, SUBLANES, LANES)
    fg3 = final_g.reshape(SUBLANES, LANES)
    rep = (jnp.arange(PEER_HALF)[None, :] // PEER_REP == jnp.arange(nsel)[:, None]).astype(BF16)
    stage = pltpu.VMEM((nsel * PEER_SLAB, LANES), I32)
    smem_idx = pl.BlockSpec((PEER_TB, nsel), lambda i: (i, 0), memory_space=pltpu.SMEM)
    tok3 = pl.BlockSpec((PEER_TB, SUBLANES, LANES), lambda i: (i, 0, 0))
    coef = pl.pallas_call(
        _peer_u_kernel,
        grid=(n // PEER_TB,),
        in_specs=[smem_idx, tok3,
                  pl.BlockSpec((PEER_TB, nsel), lambda i: (i, 0)),
                  pl.BlockSpec(rep.shape, lambda i: (0, 0)),
                  pl.BlockSpec(memory_space=pl.ANY)],
        out_specs=pl.BlockSpec((PEER_TB, PEER_HALF), lambda i: (i, 0)),
        out_shape=jax.ShapeDtypeStruct((n, PEER_HALF), F32),
        scratch_shapes=[pltpu.VMEM(u_tab.shape, I32), stage, stage,
                        pltpu.VMEM((PEER_TB, PEER_HALF), F32), pltpu.SemaphoreType.DMA(())],
        compiler_params=_params(("arbitrary",)),
    )(eidx, x3, gates, rep, u_tab)
    out3 = pl.pallas_call(
        _peer_v_kernel,
        grid=(n // PEER_TB,),
        in_specs=[smem_idx,
                  pl.BlockSpec((PEER_TB, PEER_HALF), lambda i: (i, 0)),
                  tok3,
                  pl.BlockSpec((SUBLANES, LANES), lambda i: (0, 0)),
                  pl.BlockSpec(memory_space=pl.ANY)],
        out_specs=tok3,
        out_shape=jax.ShapeDtypeStruct((n, SUBLANES, LANES), F32),
        scratch_shapes=[pltpu.VMEM(v_tab.shape, I32), stage, stage, pltpu.SemaphoreType.DMA(())],
        compiler_params=_params(("arbitrary",)),
    )(eidx, coef, x1_3, fg3, v_tab)
    return out3.reshape(n, D_MODEL)


def _permute_w_in(w_in):
    o = [0]
    for s in (GDN_WIDTH, GDN_WIDTH, GDN_WIDTH, GDN_WIDTH, GDN_HEADS, GDN_HEADS, Q_RANK, KV_RANK,
              IDX_DIM, IDX_HEADS):
        o.append(o[-1] + s)
    gq_gz = w_in[:, o[0]:o[4]]
    ga = w_in[:, o[4]:o[5]]
    gb = w_in[:, o[5]:o[6]]
    cq_ckv = w_in[:, o[6]:o[8]]
    kidx = w_in[:, o[8]:o[9]]
    widx = w_in[:, o[9]:o[10]]
    pad = jnp.zeros((w_in.shape[0], LANES - (IDX_DIM + IDX_HEADS + 2 * GDN_HEADS)), w_in.dtype)
    return jnp.concatenate([gq_gz, cq_ckv, kidx, widx, ga, gb, pad], axis=1).astype(BF16)


def kernel(x, ln1_g, w_in, conv_w, a_log, dt_bias, gdn_norm_g, q_norm_g, kv_norm_g, w_q_up,
           w_qidx_up, w_kv_up, idx_ln_g, idx_ln_b, w_out, ln2_g, peer_w_query, peer_sub_keys_1,
           peer_sub_keys_2, peer_u, peer_v, rel_bias, final_g):
    b, t, d = x.shape
    n = b * t
    assert w_in.shape[0] == 1, "single-layer block only"
    l = 0
    topk = min(IDX_TOPK_MAX, t // 4)
    bias = _bias_tiles(rel_bias)
    xc = x.reshape(n, d)
    gdn, dsa, small = _inproj(xc, ln1_g[l], _permute_w_in(w_in[l]), tm=256)
    gdn3 = gdn.reshape(b, t, -1)
    dsa3 = dsa.reshape(b, t, -1)
    small3 = small.reshape(b, t, LANES)
    o_a = _gdn(gdn3, small3, conv_w[l], a_log[l], dt_bias[l], gdn_norm_g[l], tb=256)
    q, qit, k, vt, ki, wt = _dsa_prep(dsa3, small3, q_norm_g[l], kv_norm_g[l], w_q_up[l],
                                      w_qidx_up[l], w_kv_up[l], idx_ln_g[l], idx_ln_b[l], tm=512)
    o_b = _dsa(q, qit, wt, ki, k, vt, bias, topk)
    x1, h2, eidx, gates = _mix(xc, o_a.reshape(n, -1), o_b.reshape(n, -1), w_out[l], ln2_g[l],
                               peer_w_query[l], peer_sub_keys_1[l], peer_sub_keys_2[l], tm=256)
    out = _peer(eidx, x1, h2, gates, final_g, peer_u[l], peer_v[l])
    return out.reshape(b, t, d)
```

```python
import functools
import math

import jax
import jax.numpy as jnp
from jax import lax
from jax.experimental import pallas as pl
from jax.experimental.pallas import tpu as pltpu

F32 = jnp.float32
BF16 = jnp.bfloat16
I32 = jnp.int32

D_MODEL = 1024
GDN_HEADS = 4
GDN_HEAD_DIM = 128
GDN_WIDTH = GDN_HEADS * GDN_HEAD_DIM
CONV_WIDTH = 4
GDN_CHUNK = 64
DSA_HEADS = 4
DSA_HEAD_DIM = 128
DSA_WIDTH = DSA_HEADS * DSA_HEAD_DIM
Q_RANK = 256
KV_RANK = 256
IDX_HEADS = 16
IDX_DIM = 64
IDX_TOPK_MAX = 256
REL_BUCKETS = 32
REL_MAX_DIST = 1024
PEER_HEADS = 8
PEER_KEYS = 128
PEER_KEY_DIM = 256
PEER_TOPK = 16
EPS = 1e-6

LANES = 128
SUBLANES = 8
VMEM_LIMIT = 56 * 1024 * 1024

SM_KIDX = 0
SM_WIDX = 64
SM_GA = 80
SM_GB = 84
INT_MIN = -2 ** 31


def _bdot(a, b):
    return jnp.dot(a.astype(BF16), b.astype(BF16), preferred_element_type=F32)


def _bdot_nt(a, b):
    return lax.dot_general(a.astype(BF16), b.astype(BF16), (((1,), (1,)), ((), ())),
                           preferred_element_type=F32)


def _bdot_tn(a, b):
    return lax.dot_general(a.astype(BF16), b.astype(BF16), (((0,), (0,)), ((), ())),
                           preferred_element_type=F32)


def _hdot(a, b):
    return jnp.dot(a, b, precision=lax.Precision.HIGHEST, preferred_element_type=F32)


def _rms(x, g):
    return x * lax.rsqrt(jnp.mean(x * x, axis=-1, keepdims=True) + EPS) * g


def _silu(x):
    return x * (1.0 / (1.0 + jnp.exp(-x)))


def _params(sem):
    return pltpu.CompilerParams(dimension_semantics=sem, vmem_limit_bytes=VMEM_LIMIT)


def _inproj_kernel(x_ref, g_ref, w_ref, gdn_ref, dsa_ref, small_ref):
    h = _rms(x_ref[...], g_ref[...])
    p = _bdot(h, w_ref[...])
    n_gdn = gdn_ref.shape[-1]
    n_dsa = dsa_ref.shape[-1]
    gdn_ref[...] = p[:, :n_gdn]
    dsa_ref[...] = p[:, n_gdn:n_gdn + n_dsa]
    small_ref[...] = p[:, n_gdn + n_dsa:]


def _inproj(x2, ln1_g, w_perm, tm):
    n = x2.shape[0]
    n_gdn = 4 * GDN_WIDTH
    n_dsa = Q_RANK + KV_RANK
    return pl.pallas_call(
        _inproj_kernel,
        grid=(n // tm,),
        in_specs=[pl.BlockSpec((tm, D_MODEL), lambda i: (i, 0)),
                  pl.BlockSpec((1, D_MODEL), lambda i: (0, 0)),
                  pl.BlockSpec(w_perm.shape, lambda i: (0, 0))],
        out_specs=[pl.BlockSpec((tm, n_gdn), lambda i: (i, 0)),
                   pl.BlockSpec((tm, n_dsa), lambda i: (i, 0)),
                   pl.BlockSpec((tm, LANES), lambda i: (i, 0))],
        out_shape=[jax.ShapeDtypeStruct((n, n_gdn), F32),
                   jax.ShapeDtypeStruct((n, n_dsa), F32),
                   jax.ShapeDtypeStruct((n, LANES), F32)],
        compiler_params=_params(("parallel",)),
    )(x2, ln1_g.reshape(1, D_MODEL), w_perm)


def _gdn_kernel(gdn_ref, small_ref, convw_ref, alog_ref, dtb_ref, ng_ref, o_ref,
                tail_ref, state_ref):
    tb = gdn_ref.shape[1]
    c = GDN_CHUNK
    n_chunks = tb // c
    w3 = 3 * GDN_WIDTH

    @pl.when(pl.program_id(1) == 0)
    def _():
        tail_ref[...] = jnp.zeros_like(tail_ref)
        state_ref[...] = jnp.zeros_like(state_ref)

    blk = gdn_ref[0]
    xin = blk[:, :w3]
    z = blk[:, w3:]
    tail = tail_ref[...]
    cw = convw_ref[...]
    acc = xin * cw[CONV_WIDTH - 1:CONV_WIDTH]
    rows8 = lax.broadcasted_iota(I32, (SUBLANES, w3), 0)
    for k in range(1, CONV_WIDTH):
        xk = pltpu.roll(xin, k, axis=0)
        fix = pltpu.roll(tail, k, axis=0)
        top = jnp.where(rows8 < k, fix, xk[:SUBLANES])
        xk = jnp.concatenate([top, xk[SUBLANES:]], axis=0)
        acc = acc + xk * cw[CONV_WIDTH - 1 - k:CONV_WIDTH - k]
    tail_ref[...] = xin[tb - SUBLANES:]
    qkv = _silu(acc)

    sm = small_ref[0]
    sp = sm + dtb_ref[...]
    softplus = jnp.maximum(sp, 0.0) + jnp.log(1.0 + jnp.exp(-jnp.abs(sp)))
    gl = -jnp.exp(alog_ref[...]) * softplus
    beta = 1.0 / (1.0 + jnp.exp(-sm))
    rin = lax.broadcasted_iota(I32, (tb, LANES), 0) % c
    s = 1
    while s < c:
        gl = gl + jnp.where(rin >= s, pltpu.roll(gl, s, axis=0), 0.0)
        s *= 2
    g_t = gl.T
    eg = jnp.exp(gl)

    ri = lax.broadcasted_iota(I32, (c, c), 0)
    ci = lax.broadcasted_iota(I32, (c, c), 1)
    tril = ri >= ci
    strict = ri > ci
    eye = (ri == ci).astype(F32)
    ng = ng_ref[...]
    scale = GDN_HEAD_DIM ** -0.5

    out_rows = []
    for ch in range(n_chunks):
        r0 = ch * c
        out_heads = []
        for h in range(GDN_HEADS):
            l0 = h * GDN_HEAD_DIM
            qh = qkv[r0:r0 + c, l0:l0 + GDN_HEAD_DIM]
            kh = qkv[r0:r0 + c, GDN_WIDTH + l0:GDN_WIDTH + l0 + GDN_HEAD_DIM]
            vh = qkv[r0:r0 + c, 2 * GDN_WIDTH + l0:2 * GDN_WIDTH + l0 + GDN_HEAD_DIM]
            qh = qh * lax.rsqrt(jnp.sum(qh * qh, axis=-1, keepdims=True) + EPS) * scale
            kh = kh * lax.rsqrt(jnp.sum(kh * kh, axis=-1, keepdims=True) + EPS)
            gcol = gl[r0:r0 + c, SM_GA + h:SM_GA + h + 1]
            grow = g_t[SM_GA + h:SM_GA + h + 1, r0:r0 + c]
            egcol = eg[r0:r0 + c, SM_GA + h:SM_GA + h + 1]
            bcol = beta[r0:r0 + c, SM_GB + h:SM_GB + h + 1]
            decay = jnp.exp(jnp.where(tril, gcol - grow, -jnp.inf))
            kb = kh * bcol
            vb = vh * bcol
            a_mat = jnp.where(strict, _bdot_nt(kb, kh) * decay, 0.0)
            m = -a_mat
            t_inv = eye + m
            p = 1
            while p < c // 2:
                m = _hdot(m, m)
                t_inv = t_inv + _hdot(t_inv, m)
                p *= 2
            u = _bdot(t_inv, vb)
            w = _bdot(t_inv, kb * egcol)
            attn = _bdot_nt(qh, kh) * decay
            st = state_ref[h]
            v_new = u - _bdot(w, st)
            o = _bdot(qh * egcol, st) + _bdot(attn, v_new)
            glast = gcol[c - 1:c]
            state_ref[h] = st * jnp.exp(glast) + _bdot_tn(kh * jnp.exp(glast - gcol), v_new)
            zh = z[r0:r0 + c, l0:l0 + GDN_HEAD_DIM]
            out_heads.append(_rms(o, ng) * _silu(zh))
        out_rows.append(jnp.concatenate(out_heads, axis=1))
    o_ref[0] = jnp.concatenate(out_rows, axis=0)


def _gdn(gdn3, small3, conv_w, a_log, dt_bias, norm_g, tb):
    b, t, _ = gdn3.shape
    alog_pad = jnp.zeros((1, LANES), F32).at[0, SM_GA:SM_GA + GDN_HEADS].set(a_log)
    dtb_pad = jnp.zeros((1, LANES), F32).at[0, SM_GA:SM_GA + GDN_HEADS].set(dt_bias)
    return pl.pallas_call(
        _gdn_kernel,
        grid=(b, t // tb),
        in_specs=[pl.BlockSpec((1, tb, 4 * GDN_WIDTH), lambda i, j: (i, j, 0)),
                  pl.BlockSpec((1, tb, LANES), lambda i, j: (i, j, 0)),
                  pl.BlockSpec((CONV_WIDTH, 3 * GDN_WIDTH), lambda i, j: (0, 0)),
                  pl.BlockSpec((1, LANES), lambda i, j: (0, 0)),
                  pl.BlockSpec((1, LANES), lambda i, j: (0, 0)),
                  pl.BlockSpec((1, GDN_HEAD_DIM), lambda i, j: (0, 0))],
        out_specs=pl.BlockSpec((1, tb, GDN_WIDTH), lambda i, j: (i, j, 0)),
        out_shape=jax.ShapeDtypeStruct((b, t, GDN_WIDTH), F32),
        scratch_shapes=[pltpu.VMEM((SUBLANES, 3 * GDN_WIDTH), F32),
                        pltpu.VMEM((GDN_HEADS, GDN_HEAD_DIM, GDN_HEAD_DIM), F32)],
        compiler_params=_params(("parallel", "arbitrary")),
    )(gdn3, small3, conv_w, alog_pad, dtb_pad, norm_g.reshape(1, GDN_HEAD_DIM))


def _dsa_prep_kernel(dsa_ref, small_ref, qg_ref, kvg_ref, wq_ref, wqi_t_ref, wk_ref, wv_t_ref,
                     lng_ref, lnb_ref, q_ref, qit_ref, k_ref, vt_ref, ki_ref, wt_ref):
    tm = dsa_ref.shape[1]
    blk = dsa_ref[0]
    cq = _rms(blk[:, :Q_RANK], qg_ref[...])
    ckv = _rms(blk[:, Q_RANK:], kvg_ref[...])
    q_ref[0] = _bdot(cq, wq_ref[...]).astype(BF16)
    k_ref[0] = _bdot(ckv, wk_ref[...]).astype(BF16)
    vt_ref[0] = _bdot_nt(wv_t_ref[...], ckv).astype(BF16)
    qit = _bdot_nt(wqi_t_ref[...], cq)
    nqb = tm // LANES
    cols = []
    for j in range(nqb):
        for h in range(IDX_HEADS):
            cols.append(qit[h * IDX_DIM:(h + 1) * IDX_DIM, j * LANES:(j + 1) * LANES])
    qit_ref[0] = jnp.concatenate(cols, axis=1).astype(BF16)
    sm = small_ref[0]
    kx = sm[:, SM_KIDX:SM_KIDX + IDX_DIM]
    mu = jnp.mean(kx, axis=-1, keepdims=True)
    xc = kx - mu
    kn = xc * lax.rsqrt(jnp.mean(xc * xc, axis=-1, keepdims=True) + EPS)
    ki_ref[0] = (kn * lng_ref[...] + lnb_ref[...]).astype(BF16)
    sm_t = sm.T
    wt_ref[0] = sm_t[SM_WIDX:SM_WIDX + IDX_HEADS] * (IDX_HEADS ** -0.5 * IDX_DIM ** -0.5)


def _dsa_prep(dsa3, small3, q_norm_g, kv_norm_g, w_q_up, w_qidx_up, w_kv_up, idx_ln_g, idx_ln_b, tm):
    b, t, _ = dsa3.shape
    wq = w_q_up.astype(BF16)
    wqi_t = w_qidx_up.T.astype(BF16)
    wkv = w_kv_up.reshape(KV_RANK, DSA_HEADS, 2, DSA_HEAD_DIM)
    wk = wkv[:, :, 0].reshape(KV_RANK, DSA_WIDTH).astype(BF16)
    wv_t = wkv[:, :, 1].reshape(KV_RANK, DSA_WIDTH).T.astype(BF16)
    full = lambda a: pl.BlockSpec(a.shape, lambda i, j: (0,) * a.ndim)
    qg = q_norm_g.reshape(1, Q_RANK)
    kvg = kv_norm_g.reshape(1, KV_RANK)
    lng = idx_ln_g.reshape(1, IDX_DIM)
    lnb = idx_ln_b.reshape(1, IDX_DIM)
    nqw = IDX_HEADS * LANES
    return pl.pallas_call(
        _dsa_prep_kernel,
        grid=(b, t // tm),
        in_specs=[pl.BlockSpec((1, tm, Q_RANK + KV_RANK), lambda i, j: (i, j, 0)),
                  pl.BlockSpec((1, tm, LANES), lambda i, j: (i, j, 0)),
                  full(qg), full(kvg), full(wq), full(wqi_t), full(wk), full(wv_t),
                  full(lng), full(lnb)],
        out_specs=[pl.BlockSpec((1, tm, DSA_WIDTH), lambda i, j: (i, j, 0)),
                   pl.BlockSpec((1, IDX_DIM, (tm // LANES) * nqw), lambda i, j: (i, 0, j)),
                   pl.BlockSpec((1, tm, DSA_WIDTH), lambda i, j: (i, j, 0)),
                   pl.BlockSpec((1, DSA_WIDTH, tm), lambda i, j: (i, 0, j)),
                   pl.BlockSpec((1, tm, IDX_DIM), lambda i, j: (i, j, 0)),
                   pl.BlockSpec((1, IDX_HEADS, tm), lambda i, j: (i, 0, j))],
        out_shape=[jax.ShapeDtypeStruct((b, t, DSA_WIDTH), BF16),
                   jax.ShapeDtypeStruct((b, IDX_DIM, (t // LANES) * nqw), BF16),
                   jax.ShapeDtypeStruct((b, t, DSA_WIDTH), BF16),
                   jax.ShapeDtypeStruct((b, DSA_WIDTH, t), BF16),
                   jax.ShapeDtypeStruct((b, t, IDX_DIM), BF16),
                   jax.ShapeDtypeStruct((b, IDX_HEADS, t), F32)],
        compiler_params=_params(("parallel", "parallel")),
    )(dsa3, small3, qg, kvg, wq, wqi_t, wk, wv_t, lng, lnb)


DSA_QB = 128
DSA_KC = 512
DSA_AC = DSA_KC
N_BIAS_TILES = 9


def _dsa_kernel(q_ref, qit_ref, wt_ref, ki_ref, k_ref, vt_ref, bias_ref, o_ref,
                keys_ref, *acc_refs, topk):
    i = pl.program_id(1)
    t0 = i * DSA_QB
    n_kc = (t0 + DSA_QB + DSA_KC - 1) // DSA_KC
    tq = t0 + lax.broadcasted_iota(I32, (1, DSA_QB), 1)
    qit = qit_ref[0]
    wt = wt_ref[0]

    def score_chunk(kc, carry):
        r0 = pl.multiple_of(kc * DSA_KC, DSA_KC)
        big = jnp.dot(ki_ref[0, pl.ds(r0, DSA_KC), :], qit, preferred_element_type=F32)
        sc = jnp.zeros((DSA_KC, DSA_QB), F32)
        for h in range(IDX_HEADS):
            sc = sc + jnp.maximum(big[:, h * DSA_QB:(h + 1) * DSA_QB], 0.0) * wt[h:h + 1]
        bits = pltpu.bitcast(sc, I32)
        key = bits ^ ((bits >> 31) & 0x7FFFFFFF)
        spos = r0 + lax.broadcasted_iota(I32, (DSA_KC, 1), 0)
        keys_ref[pl.ds(r0, DSA_KC), :] = jnp.where(spos <= tq, key, INT_MIN)
        return carry

    lax.fori_loop(0, n_kc, score_chunk, 0)

    def count_ge(cand):
        def body(kc, cnt):
            r0 = pl.multiple_of(kc * DSA_KC, DSA_KC)
            ge = (keys_ref[pl.ds(r0, DSA_KC), :] >= cand).astype(I32)
            return cnt + jnp.sum(ge.reshape(DSA_KC // SUBLANES, SUBLANES, DSA_QB), axis=0)
        cnt8 = lax.fori_loop(0, n_kc, body, jnp.zeros((SUBLANES, DSA_QB), I32))
        return jnp.sum(cnt8, axis=0, keepdims=True)

    zero = jnp.zeros((1, DSA_QB), I32)
    ans = jnp.where(count_ge(zero) >= topk, zero, jnp.full((1, DSA_QB), INT_MIN, I32))

    def bit_pass(b, ans):
        cand = ans | (jnp.int32(1) << (30 - b))
        return jnp.where(count_ge(cand) >= topk, cand, ans)

    ans = lax.fori_loop(0, 31, bit_pass, ans)
    thr = jnp.maximum(ans, INT_MIN + 1)

    for h in range(DSA_HEADS):
        acc_refs[h][...] = jnp.zeros((DSA_HEAD_DIM, DSA_QB), F32)
    qb = q_ref[0]
    scale = DSA_HEAD_DIM ** -0.5
    sub = DSA_AC // DSA_QB

    def attend(kc, carry):
        ms, ls = carry
        r0 = pl.multiple_of(kc * DSA_AC, DSA_AC)
        sel = keys_ref[pl.ds(r0, DSA_AC), :] >= thr
        tiles = [jnp.clip(i - (kc * sub + a), 0, N_BIAS_TILES - 1) for a in range(sub)]
        new_ms, new_ls = [], []
        for h in range(DSA_HEADS):
            l0 = h * DSA_HEAD_DIM
            s_t = lax.dot_general(k_ref[0, pl.ds(r0, DSA_AC), l0:l0 + DSA_HEAD_DIM],
                                  qb[:, l0:l0 + DSA_HEAD_DIM], (((1,), (1,)), ((), ())),
                                  preferred_element_type=F32)
            bias = jnp.concatenate([bias_ref[h, bt] for bt in tiles], axis=0)
            logit = jnp.where(sel, s_t * scale + bias, -jnp.inf)
            m_new = jnp.maximum(ms[h], jnp.max(logit, axis=0, keepdims=True))
            m_safe = jnp.where(m_new == -jnp.inf, 0.0, m_new)
            alpha = jnp.exp(ms[h] - m_safe)
            p = jnp.exp(logit - m_safe)
            new_ls.append(alpha * ls[h] + jnp.sum(p, axis=0, keepdims=True))
            new_ms.append(m_new)
            pv = jnp.dot(vt_ref[0, l0:l0 + DSA_HEAD_DIM, pl.ds(r0, DSA_AC)], p.astype(BF16),
                         preferred_element_type=F32)
            acc_refs[h][...] = acc_refs[h][...] * alpha + pv
        return tuple(new_ms), tuple(new_ls)

    m0 = tuple(jnp.full((1, DSA_QB), -jnp.inf, F32) for _ in range(DSA_HEADS))
    l0s = tuple(jnp.zeros((1, DSA_QB), F32) for _ in range(DSA_HEADS))
    _, ls = lax.fori_loop(0, n_kc, attend, (m0, l0s))
    o_ref[0] = jnp.concatenate([(acc_refs[h][...] / ls[h]).T for h in range(DSA_HEADS)], axis=1)


def _rel_bucket(dist):
    max_exact = REL_BUCKETS // 2
    n = jnp.maximum(dist, 0)
    nf = jnp.maximum(n, 1).astype(F32)
    large = max_exact + (jnp.log(nf / max_exact) / math.log(REL_MAX_DIST / max_exact)
                         * (REL_BUCKETS - max_exact)).astype(jnp.int32)
    large = jnp.minimum(large, REL_BUCKETS - 1)
    return jnp.where(n < max_exact, n, large)


def _bias_tiles(rel_bias):
    bt = DSA_QB
    span = 2 * bt
    dist = jnp.arange(-(bt - 1), N_BIAS_TILES * bt + 1)
    by_dist = rel_bias[_rel_bucket(dist)].astype(F32).T
    rows = jnp.stack([by_dist[:, j * bt:j * bt + span] for j in range(N_BIAS_TILES)], axis=1)
    rep = jnp.tile(rows, (1, 1, bt))[..., bt - 1:bt - 1 + bt * (span - 1)]
    return rep.reshape(DSA_HEADS, N_BIAS_TILES, bt, span - 1)[..., :bt]


def _dsa(q, qit, wt, ki, k, vt, bias, topk):
    b, t, _ = q.shape
    nqw = IDX_HEADS * DSA_QB
    t_pad = ((t + DSA_KC - 1) // DSA_KC) * DSA_KC
    kern = functools.partial(_dsa_kernel, topk=topk)
    return pl.pallas_call(
        kern,
        grid=(b, t // DSA_QB),
        in_specs=[pl.BlockSpec((1, DSA_QB, DSA_WIDTH), lambda i, j: (i, j, 0)),
                  pl.BlockSpec((1, IDX_DIM, nqw), lambda i, j: (i, 0, j)),
                  pl.BlockSpec((1, IDX_HEADS, DSA_QB), lambda i, j: (i, 0, j)),
                  pl.BlockSpec((1, t, IDX_DIM), lambda i, j: (i, 0, 0)),
                  pl.BlockSpec((1, t, DSA_WIDTH), lambda i, j: (i, 0, 0)),
                  pl.BlockSpec((1, DSA_WIDTH, t), lambda i, j: (i, 0, 0)),
                  pl.BlockSpec(bias.shape, lambda i, j: (0, 0, 0, 0))],
        out_specs=pl.BlockSpec((1, DSA_QB, DSA_WIDTH), lambda i, j: (i, j, 0)),
        out_shape=jax.ShapeDtypeStruct((b, t, DSA_WIDTH), F32),
        scratch_shapes=[pltpu.VMEM((t_pad, DSA_QB), I32)]
        + [pltpu.VMEM((DSA_HEAD_DIM, DSA_QB), F32)] * DSA_HEADS,
        compiler_params=_params(("parallel", "arbitrary")),
    )(q, qit, wt, ki, k, vt, bias)


def _top16_rows(s, n_rows):
    tm = s.shape[1]
    rid = lax.broadcasted_iota(I32, (n_rows, tm), 0)
    vals, idxs = [], []
    for _ in range(PEER_TOPK):
        m = jnp.max(s, axis=0, keepdims=True)
        ix = jnp.min(jnp.where(s == m, rid, n_rows), axis=0, keepdims=True)
        vals.append(m)
        idxs.append(ix)
        s = jnp.where(rid == ix, -jnp.inf, s)
    return jnp.concatenate(vals, axis=0), jnp.concatenate(idxs, axis=0)


def _mix_kernel(x_ref, oa_ref, ob_ref, wo_ref, g2_ref, wq_ref, sk1_ref, sk2_ref,
                x1_ref, h2_ref, eidx_ref, gates_ref):
    tm = x_ref.shape[0]
    o = jnp.concatenate([oa_ref[...], ob_ref[...]], axis=1)
    x1 = x_ref[...] + _bdot(o, wo_ref[...])
    x1_ref[...] = x1
    h2 = _rms(x1, g2_ref[...])
    h2_ref[...] = h2
    query = _bdot(h2, wq_ref[...])
    half = PEER_KEY_DIM // 2
    nk2 = PEER_TOPK * PEER_TOPK
    pid = lax.broadcasted_iota(I32, (nk2, tm), 0)
    e_rows, g_rows = [], []
    for h in range(PEER_HEADS):
        q1 = query[:, h * PEER_KEY_DIM:h * PEER_KEY_DIM + half]
        q2 = query[:, h * PEER_KEY_DIM + half:(h + 1) * PEER_KEY_DIM]
        s1 = _bdot_nt(sk1_ref[h], q1)
        s2 = _bdot_nt(sk2_ref[h], q2)
        v1, i1 = _top16_rows(s1, PEER_KEYS)
        v2, i2 = _top16_rows(s2, PEER_KEYS)
        cand = jnp.concatenate([v1[a:a + 1] + v2 for a in range(PEER_TOPK)], axis=0)
        cidx = jnp.concatenate([i1[a:a + 1] * PEER_KEYS + i2 for a in range(PEER_TOPK)], axis=0)
        tops, eids = [], []
        for _ in range(PEER_TOPK):
            m = jnp.max(cand, axis=0, keepdims=True)
            pos = jnp.min(jnp.where(cand == m, pid, nk2), axis=0, keepdims=True)
            hit = pid == pos
            eids.append(jnp.max(jnp.where(hit, cidx, -1), axis=0, keepdims=True))
            tops.append(m)
            cand = jnp.where(hit, -jnp.inf, cand)
        top_s = jnp.concatenate(tops, axis=0)
        ex = jnp.exp(top_s - top_s[0:1])
        g_rows.append(ex / jnp.sum(ex, axis=0, keepdims=True))
        e_rows.append(jnp.concatenate(eids, axis=0))
    gates_ref[...] = jnp.concatenate(g_rows, axis=0).T
    eidx_ref[...] = (jnp.concatenate(e_rows, axis=0) * PEER_SLAB).T


def _mix(x2, o_a, o_b, w_out, ln2_g, w_query, sk1, sk2, tm):
    n = x2.shape[0]
    wo = w_out.astype(BF16)
    wq = w_query.astype(BF16)
    sk1 = sk1.astype(BF16)
    sk2 = sk2.astype(BF16)
    nsel = PEER_HEADS * PEER_TOPK
    row = lambda w: pl.BlockSpec((tm, w), lambda i: (i, 0))
    full = lambda a: pl.BlockSpec(a.shape, lambda i: (0,) * a.ndim)
    g2 = ln2_g.reshape(1, D_MODEL)
    return pl.pallas_call(
        _mix_kernel,
        grid=(n // tm,),
        in_specs=[row(D_MODEL), row(GDN_WIDTH), row(DSA_WIDTH), full(wo), full(g2), full(wq),
                  full(sk1), full(sk2)],
        out_specs=[row(D_MODEL), row(D_MODEL), row(nsel), row(nsel)],
        out_shape=[jax.ShapeDtypeStruct((n, D_MODEL), F32),
                   jax.ShapeDtypeStruct((n, D_MODEL), F32),
                   jax.ShapeDtypeStruct((n, nsel), I32),
                   jax.ShapeDtypeStruct((n, nsel), F32)],
        compiler_params=_params(("parallel",)),
    )(x2, o_a, o_b, wo, g2, wq, sk1, sk2)


PEER_TB = 32
PEER_SLAB = 4
PEER_HALF = D_MODEL // 2
PEER_REP = PEER_HALF // (PEER_HEADS * PEER_TOPK)


def _pack_table(w):
    e = w.shape[0]
    bits = lax.bitcast_convert_type(w.astype(jnp.bfloat16), jnp.uint16).astype(jnp.uint32)
    words = bits[:, :PEER_HALF] | (bits[:, PEER_HALF:] << 16)
    return lax.bitcast_convert_type(words, I32).reshape(e * PEER_SLAB, LANES)


def _unpack_slabs(words):
    lo = lax.bitcast_convert_type(words << 16, F32)
    hi = lax.bitcast_convert_type(words & jnp.int32(-65536), F32)
    return lo, hi


def _split_bf16(x):
    head = x.astype(jnp.bfloat16).astype(F32)
    return head, x - head


def _gather_slabs(eidx_ref, tok, tab_ref, stage_ref):
    nsel = PEER_HEADS * PEER_TOPK
    for k in range(nsel):
        row = pl.multiple_of(eidx_ref[tok, k], PEER_SLAB)
        stage_ref[k * PEER_SLAB:(k + 1) * PEER_SLAB, :] = tab_ref[pl.ds(row, PEER_SLAB), :]


def _peer_token_loop(eidx_ref, tab_ref, stages, compute):
    for tok in range(PEER_TB):
        stage_ref = stages[tok % len(stages)]
        _gather_slabs(eidx_ref, tok, tab_ref, stage_ref)
        compute(tok, stage_ref)


def _load_table(tab_hbm, tab_ref, sem):
    @pl.when(pl.program_id(0) == 0)
    def _():
        cp = pltpu.make_async_copy(tab_hbm, tab_ref, sem)
        cp.start()
        cp.wait()


def _slab_mask():
    j = lax.broadcasted_iota(I32, (SUBLANES, PEER_HALF), 0)
    col = lax.broadcasted_iota(I32, (SUBLANES, PEER_HALF), 1)
    return (col % PEER_SLAB) == (j % PEER_SLAB)


def _peer_u_kernel(eidx_ref, x3_ref, gates_ref, rep_ref, tab_hbm, c_ref, tab_ref, st_a, st_b, z_ref, sem):
    _load_table(tab_hbm, tab_ref, sem)
    mask = _slab_mask()
    lane = lax.broadcasted_iota(I32, (SUBLANES, PEER_HALF), 1)

    def compute(tok, stage_ref):
        lo, hi = _unpack_slabs(stage_ref[...])
        xh, xt = _split_bf16(x3_ref[tok])
        a_lo = jnp.concatenate([xh[:PEER_SLAB], xt[:PEER_SLAB]], axis=0)
        a_hi = jnp.concatenate([xh[PEER_SLAB:], xt[PEER_SLAB:]], axis=0)
        r = _bdot_nt(a_lo, lo) + _bdot_nt(a_hi, hi)
        z_ref[pl.ds(tok, 1), :] = jnp.sum(jnp.where(mask, r, 0.0), axis=0, keepdims=True)

    _peer_token_loop(eidx_ref, tab_ref, (st_a, st_b), compute)

    gh, gt = _split_bf16(gates_ref[...])
    rep = rep_ref[...]
    gates_rep = _bdot(gh, rep) + _bdot(gt, rep)
    for g in range(PEER_TB // SUBLANES):
        z = z_ref[g * SUBLANES:(g + 1) * SUBLANES, :]
        s = 1
        while s < PEER_SLAB:
            up = pltpu.roll(z, PEER_HALF - s, axis=1)
            dn = pltpu.roll(z, s, axis=1)
            z = z + jnp.where((lane // s) % 2 == 0, up, dn)
            s *= 2
        gelu = 0.5 * z * (1.0 + lax.erf(z * (2.0 ** -0.5)))
        c_ref[g * SUBLANES:(g + 1) * SUBLANES, :] = gelu * gates_rep[g * SUBLANES:(g + 1) * SUBLANES]


def _peer_v_kernel(eidx_ref, c_in_ref, x1_ref, fg_ref, tab_hbm, o_ref, tab_ref, st_a, st_b, sem):
    _load_table(tab_hbm, tab_ref, sem)
    mask = _slab_mask()

    def compute(tok, stage_ref):
        lo, hi = _unpack_slabs(stage_ref[...])
        ch, ct = _split_bf16(c_in_ref[pl.ds(tok, 1), :])
        rows = lax.broadcasted_iota(I32, (SUBLANES, PEER_HALF), 0)
        c8 = jnp.where(mask, jnp.where(rows < PEER_SLAB, ch, ct), 0.0)
        y_lo = _bdot(c8, lo)
        y_hi = _bdot(c8, hi)
        y = jnp.concatenate([y_lo[:PEER_SLAB] + y_lo[PEER_SLAB:], y_hi[:PEER_SLAB] + y_hi[PEER_SLAB:]],
                            axis=0)
        o_ref[tok] = y

    _peer_token_loop(eidx_ref, tab_ref, (st_a, st_b), compute)
    zt = x1_ref[...] + o_ref[...]
    ms = jnp.sum(jnp.sum(zt * zt, axis=2, keepdims=True), axis=1, keepdims=True) * (1.0 / D_MODEL)
    o_ref[...] = zt * lax.rsqrt(ms + EPS) * fg_ref[...]


def _peer(eidx, x1, h2, gates, final_g, peer_u, peer_v):
    n = x1.shape[0]
    nsel = PEER_HEADS * PEER_TOPK
    u_tab = _pack_table(peer_u)
    v_tab = _pack_table(peer_v)
    x3 = h2.reshape(n, SUBLANES, LANES)
    x1_3 = x1.reshape(n, SUBLANES, LANES)
    fg3 = final_g.reshape(SUBLANES, LANES)
    rep = (jnp.arange(PEER_HALF)[None, :] // PEER_REP == jnp.arange(nsel)[:, None]).astype(BF16)
    stage = pltpu.VMEM((nsel * PEER_SLAB, LANES), I32)
    smem_idx = pl.BlockSpec((PEER_TB, nsel), lambda i: (i, 0), memory_space=pltpu.SMEM)
    tok3 = pl.BlockSpec((PEER_TB, SUBLANES, LANES), lambda i: (i, 0, 0))
    coef = pl.pallas_call(
        _peer_u_kernel,
        grid=(n // PEER_TB,),
        in_specs=[smem_idx, tok3,
                  pl.BlockSpec((PEER_TB, nsel), lambda i: (i, 0)),
                  pl.BlockSpec(rep.shape, lambda i: (0, 0)),
                  pl.BlockSpec(memory_space=pl.ANY)],
        out_specs=pl.BlockSpec((PEER_TB, PEER_HALF), lambda i: (i, 0)),
        out_shape=jax.ShapeDtypeStruct((n, PEER_HALF), F32),
        scratch_shapes=[pltpu.VMEM(u_tab.shape, I32), stage, stage,
                        pltpu.VMEM((PEER_TB, PEER_HALF), F32), pltpu.SemaphoreType.DMA(())],
        compiler_params=_params(("arbitrary",)),
    )(eidx, x3, gates, rep, u_tab)
    out3 = pl.pallas_call(
        _peer_v_kernel,
        grid=(n // PEER_TB,),
        in_specs=[smem_idx,
                  pl.BlockSpec((PEER_TB, PEER_HALF), lambda i: (i, 0)),
                  tok3,
                  pl.BlockSpec((SUBLANES, LANES), lambda i: (0, 0)),
                  pl.BlockSpec(memory_space=pl.ANY)],
        out_specs=tok3,
        out_shape=jax.ShapeDtypeStruct((n, SUBLANES, LANES), F32),
        scratch_shapes=[pltpu.VMEM(v_tab.shape, I32), stage, stage, pltpu.SemaphoreType.DMA(())],
        compiler_params=_params(("arbitrary",)),
    )(eidx, coef, x1_3, fg3, v_tab)
    return out3.reshape(n, D_MODEL)


def _permute_w_in(w_in):
    o = [0]
    for s in (GDN_WIDTH, GDN_WIDTH, GDN_WIDTH, GDN_WIDTH, GDN_HEADS, GDN_HEADS, Q_RANK, KV_RANK,
              IDX_DIM, IDX_HEADS):
        o.append(o[-1] + s)
    gq_gz = w_in[:, o[0]:o[4]]
    ga = w_in[:, o[4]:o[5]]
    gb = w_in[:, o[5]:o[6]]
    cq_ckv = w_in[:, o[6]:o[8]]
    kidx = w_in[:, o[8]:o[9]]
    widx = w_in[:, o[9]:o[10]]
    pad = jnp.zeros((w_in.shape[0], LANES - (IDX_DIM + IDX_HEADS + 2 * GDN_HEADS)), w_in.dtype)
    return jnp.concatenate([gq_gz, cq_ckv, kidx, widx, ga, gb, pad], axis=1).astype(BF16)


def kernel(x, ln1_g, w_in, conv_w, a_log, dt_bias, gdn_norm_g, q_norm_g, kv_norm_g, w_q_up,
           w_qidx_up, w_kv_up, idx_ln_g, idx_ln_b, w_out, ln2_g, peer_w_query, peer_sub_keys_1,
           peer_sub_keys_2, peer_u, peer_v, rel_bias, final_g):
    b, t, d = x.shape
    n = b * t
    assert w_in.shape[0] == 1, "single-layer block only"
    l = 0
    topk = min(IDX_TOPK_MAX, t // 4)
    bias = _bias_tiles(rel_bias)
    xc = x.reshape(n, d)
    gdn, dsa, small = _inproj(xc, ln1_g[l], _permute_w_in(w_in[l]), tm=256)
    gdn3 = gdn.reshape(b, t, -1)
    dsa3 = dsa.reshape(b, t, -1)
    small3 = small.reshape(b, t, LANES)
    o_a = _gdn(gdn3, small3, conv_w[l], a_log[l], dt_bias[l], gdn_norm_g[l], tb=256)
    q, qit, k, vt, ki, wt = _dsa_prep(dsa3, small3, q_norm_g[l], kv_norm_g[l], w_q_up[l],
                                      w_qidx_up[l], w_kv_up[l], idx_ln_g[l], idx_ln_b[l], tm=512)
    o_b = _dsa(q, qit, wt, ki, k, vt, bias, topk)
    x1, h2, eidx, gates = _mix(xc, o_a.reshape(n, -1), o_b.reshape(n, -1), w_out[l], ln2_g[l],
                               peer_w_query[l], peer_sub_keys_1[l], peer_sub_keys_2[l], tm=256)
    out = _peer(eidx, x1, h2, gates, final_g, peer_u[l], peer_v[l])
    return out.reshape(b, t, d)
```

```python
import functools
import math

import jax
import jax.numpy as jnp
from jax import lax
from jax.experimental import pallas as pl
from jax.experimental.pallas import tpu as pltpu

F32 = jnp.float32
BF16 = jnp.bfloat16
I32 = jnp.int32

D_MODEL = 1024
GDN_HEADS = 4
GDN_HEAD_DIM = 128
GDN_WIDTH = GDN_HEADS * GDN_HEAD_DIM
CONV_WIDTH = 4
GDN_CHUNK = 64
DSA_HEADS = 4
DSA_HEAD_DIM = 128
DSA_WIDTH = DSA_HEADS * DSA_HEAD_DIM
Q_RANK = 256
KV_RANK = 256
IDX_HEADS = 16
IDX_DIM = 64
IDX_TOPK_MAX = 256
REL_BUCKETS = 32
REL_MAX_DIST = 1024
PEER_HEADS = 8
PEER_KEYS = 128
PEER_KEY_DIM = 256
PEER_TOPK = 16
EPS = 1e-6

LANES = 128
SUBLANES = 8
VMEM_LIMIT = 56 * 1024 * 1024

SM_KIDX = 0
SM_WIDX = 64
SM_GA = 80
SM_GB = 84
INT_MIN = -2 ** 31


def _bdot(a, b):
    return jnp.dot(a.astype(BF16), b.astype(BF16), preferred_element_type=F32)


def _bdot_nt(a, b):
    return lax.dot_general(a.astype(BF16), b.astype(BF16), (((1,), (1,)), ((), ())),
                           preferred_element_type=F32)


def _bdot_tn(a, b):
    return lax.dot_general(a.astype(BF16), b.astype(BF16), (((0,), (0,)), ((), ())),
                           preferred_element_type=F32)


def _hdot(a, b):
    return jnp.dot(a, b, precision=lax.Precision.HIGHEST, preferred_element_type=F32)


def _rms(x, g):
    return x * lax.rsqrt(jnp.mean(x * x, axis=-1, keepdims=True) + EPS) * g


def _silu(x):
    return x * (1.0 / (1.0 + jnp.exp(-x)))


def _params(sem):
    return pltpu.CompilerParams(dimension_semantics=sem, vmem_limit_bytes=VMEM_LIMIT)


def _inproj_kernel(x_ref, g_ref, w_ref, gdn_ref, dsa_ref, small_ref):
    h = _rms(x_ref[...], g_ref[...])
    p = _bdot(h, w_ref[...])
    n_gdn = gdn_ref.shape[-1]
    n_dsa = dsa_ref.shape[-1]
    gdn_ref[...] = p[:, :n_gdn]
    dsa_ref[...] = p[:, n_gdn:n_gdn + n_dsa]
    small_ref[...] = p[:, n_gdn + n_dsa:]


def _inproj(x2, ln1_g, w_perm, tm):
    n = x2.shape[0]
    n_gdn = 4 * GDN_WIDTH
    n_dsa = Q_RANK + KV_RANK
    return pl.pallas_call(
        _inproj_kernel,
        grid=(n // tm,),
        in_specs=[pl.BlockSpec((tm, D_MODEL), lambda i: (i, 0)),
                  pl.BlockSpec((1, D_MODEL), lambda i: (0, 0)),
                  pl.BlockSpec(w_perm.shape, lambda i: (0, 0))],
        out_specs=[pl.BlockSpec((tm, n_gdn), lambda i: (i, 0)),
                   pl.BlockSpec((tm, n_dsa), lambda i: (i, 0)),
                   pl.BlockSpec((tm, LANES), lambda i: (i, 0))],
        out_shape=[jax.ShapeDtypeStruct((n, n_gdn), F32),
                   jax.ShapeDtypeStruct((n, n_dsa), F32),
                   jax.ShapeDtypeStruct((n, LANES), F32)],
        compiler_params=_params(("parallel",)),
    )(x2, ln1_g.reshape(1, D_MODEL), w_perm)


def _gdn_kernel(gdn_ref, small_ref, convw_ref, alog_ref, dtb_ref, ng_ref, o_ref,
                tail_ref, state_ref):
    tb = gdn_ref.shape[1]
    c = GDN_CHUNK
    n_chunks = tb // c
    w3 = 3 * GDN_WIDTH

    @pl.when(pl.program_id(1) == 0)
    def _():
        tail_ref[...] = jnp.zeros_like(tail_ref)
        state_ref[...] = jnp.zeros_like(state_ref)

    blk = gdn_ref[0]
    xin = blk[:, :w3]
    z = blk[:, w3:]
    tail = tail_ref[...]
    cw = convw_ref[...]
    acc = xin * cw[CONV_WIDTH - 1:CONV_WIDTH]
    rows8 = lax.broadcasted_iota(I32, (SUBLANES, w3), 0)
    for k in range(1, CONV_WIDTH):
        xk = pltpu.roll(xin, k, axis=0)
        fix = pltpu.roll(tail, k, axis=0)
        top = jnp.where(rows8 < k, fix, xk[:SUBLANES])
        xk = jnp.concatenate([top, xk[SUBLANES:]], axis=0)
        acc = acc + xk * cw[CONV_WIDTH - 1 - k:CONV_WIDTH - k]
    tail_ref[...] = xin[tb - SUBLANES:]
    qkv = _silu(acc)

    sm = small_ref[0]
    sp = sm + dtb_ref[...]
    softplus = jnp.maximum(sp, 0.0) + jnp.log(1.0 + jnp.exp(-jnp.abs(sp)))
    gl = -jnp.exp(alog_ref[...]) * softplus
    beta = 1.0 / (1.0 + jnp.exp(-sm))
    rin = lax.broadcasted_iota(I32, (tb, LANES), 0) % c
    s = 1
    while s < c:
        gl = gl + jnp.where(rin >= s, pltpu.roll(gl, s, axis=0), 0.0)
        s *= 2
    g_t = gl.T
    eg = jnp.exp(gl)

    ri = lax.broadcasted_iota(I32, (c, c), 0)
    ci = lax.broadcasted_iota(I32, (c, c), 1)
    tril = ri >= ci
    strict = ri > ci
    eye = (ri == ci).astype(F32)
    ng = ng_ref[...]
    scale = GDN_HEAD_DIM ** -0.5

    probs = [(ch, h) for ch in range(n_chunks) for h in range(GDN_HEADS)]
    qs, ks, gcols, egcols, decays, kbs, vbs, ms, t_invs = ({} for _ in range(9))
    for ch, h in probs:
        r0, l0 = ch * c, h * GDN_HEAD_DIM
        qh = qkv[r0:r0 + c, l0:l0 + GDN_HEAD_DIM]
        kh = qkv[r0:r0 + c, GDN_WIDTH + l0:GDN_WIDTH + l0 + GDN_HEAD_DIM]
        vh = qkv[r0:r0 + c, 2 * GDN_WIDTH + l0:2 * GDN_WIDTH + l0 + GDN_HEAD_DIM]
        qs[ch, h] = qh * lax.rsqrt(jnp.sum(qh * qh, axis=-1, keepdims=True) + EPS) * scale
        kh = kh * lax.rsqrt(jnp.sum(kh * kh, axis=-1, keepdims=True) + EPS)
        ks[ch, h] = kh
        gcol = gl[r0:r0 + c, SM_GA + h:SM_GA + h + 1]
        grow = g_t[SM_GA + h:SM_GA + h + 1, r0:r0 + c]
        gcols[ch, h] = gcol
        egcols[ch, h] = eg[r0:r0 + c, SM_GA + h:SM_GA + h + 1]
        bcol = beta[r0:r0 + c, SM_GB + h:SM_GB + h + 1]
        decays[ch, h] = jnp.exp(jnp.where(tril, gcol - grow, -jnp.inf))
        kbs[ch, h] = kh * bcol
        vbs[ch, h] = vh * bcol
    for p in probs:
        a_mat = jnp.where(strict, _bdot_nt(kbs[p], ks[p]) * decays[p], 0.0)
        ms[p] = -a_mat
        t_invs[p] = eye + ms[p]
    step = 1
    while step < c // 2:
        for p in probs:
            ms[p] = _hdot(ms[p], ms[p])
        for p in probs:
            t_invs[p] = t_invs[p] + _hdot(t_invs[p], ms[p])
        step *= 2
    us = {p: _bdot(t_invs[p], vbs[p]) for p in probs}
    ws = {p: _bdot(t_invs[p], kbs[p] * egcols[p]) for p in probs}
    attns = {p: _bdot_nt(qs[p], ks[p]) * decays[p] for p in probs}

    out_rows = []
    for ch in range(n_chunks):
        r0 = ch * c
        out_heads = []
        for h in range(GDN_HEADS):
            p = (ch, h)
            l0 = h * GDN_HEAD_DIM
            st = state_ref[h]
            v_new = us[p] - _bdot(ws[p], st)
            o = _bdot(qs[p] * egcols[p], st) + _bdot(attns[p], v_new)
            glast = gcols[p][c - 1:c]
            state_ref[h] = st * jnp.exp(glast) + _bdot_tn(ks[p] * jnp.exp(glast - gcols[p]), v_new)
            zh = z[r0:r0 + c, l0:l0 + GDN_HEAD_DIM]
            out_heads.append(_rms(o, ng) * _silu(zh))
        out_rows.append(jnp.concatenate(out_heads, axis=1))
    o_ref[0] = jnp.concatenate(out_rows, axis=0)


def _gdn(gdn3, small3, conv_w, a_log, dt_bias, norm_g, tb):
    b, t, _ = gdn3.shape
    alog_pad = jnp.zeros((1, LANES), F32).at[0, SM_GA:SM_GA + GDN_HEADS].set(a_log)
    dtb_pad = jnp.zeros((1, LANES), F32).at[0, SM_GA:SM_GA + GDN_HEADS].set(dt_bias)
    return pl.pallas_call(
        _gdn_kernel,
        grid=(b, t // tb),
        in_specs=[pl.BlockSpec((1, tb, 4 * GDN_WIDTH), lambda i, j: (i, j, 0)),
                  pl.BlockSpec((1, tb, LANES), lambda i, j: (i, j, 0)),
                  pl.BlockSpec((CONV_WIDTH, 3 * GDN_WIDTH), lambda i, j: (0, 0)),
                  pl.BlockSpec((1, LANES), lambda i, j: (0, 0)),
                  pl.BlockSpec((1, LANES), lambda i, j: (0, 0)),
                  pl.BlockSpec((1, GDN_HEAD_DIM), lambda i, j: (0, 0))],
        out_specs=pl.BlockSpec((1, tb, GDN_WIDTH), lambda i, j: (i, j, 0)),
        out_shape=jax.ShapeDtypeStruct((b, t, GDN_WIDTH), F32),
        scratch_shapes=[pltpu.VMEM((SUBLANES, 3 * GDN_WIDTH), F32),
                        pltpu.VMEM((GDN_HEADS, GDN_HEAD_DIM, GDN_HEAD_DIM), F32)],
        compiler_params=_params(("parallel", "arbitrary")),
    )(gdn3, small3, conv_w, alog_pad, dtb_pad, norm_g.reshape(1, GDN_HEAD_DIM))


def _dsa_prep_kernel(dsa_ref, small_ref, qg_ref, kvg_ref, wq_ref, wqi_t_ref, wk_ref, wv_t_ref,
                     lng_ref, lnb_ref, q_ref, qit_ref, k_ref, vt_ref, ki_ref, wt_ref):
    tm = dsa_ref.shape[1]
    blk = dsa_ref[0]
    cq = _rms(blk[:, :Q_RANK], qg_ref[...])
    ckv = _rms(blk[:, Q_RANK:], kvg_ref[...])
    q_ref[0] = _bdot(cq, wq_ref[...]).astype(BF16)
    k_ref[0] = _bdot(ckv, wk_ref[...]).astype(BF16)
    vt_ref[0] = _bdot_nt(wv_t_ref[...], ckv).astype(BF16)
    qit = _bdot_nt(wqi_t_ref[...], cq)
    nqb = tm // LANES
    cols = []
    for j in range(nqb):
        for h in range(IDX_HEADS):
            cols.append(qit[h * IDX_DIM:(h + 1) * IDX_DIM, j * LANES:(j + 1) * LANES])
    qit_ref[0] = jnp.concatenate(cols, axis=1).astype(BF16)
    sm = small_ref[0]
    kx = sm[:, SM_KIDX:SM_KIDX + IDX_DIM]
    mu = jnp.mean(kx, axis=-1, keepdims=True)
    xc = kx - mu
    kn = xc * lax.rsqrt(jnp.mean(xc * xc, axis=-1, keepdims=True) + EPS)
    ki_ref[0] = (kn * lng_ref[...] + lnb_ref[...]).astype(BF16)
    sm_t = sm.T
    wt_ref[0] = sm_t[SM_WIDX:SM_WIDX + IDX_HEADS] * (IDX_HEADS ** -0.5 * IDX_DIM ** -0.5)


def _dsa_prep(dsa3, small3, q_norm_g, kv_norm_g, w_q_up, w_qidx_up, w_kv_up, idx_ln_g, idx_ln_b, tm):
    b, t, _ = dsa3.shape
    wq = w_q_up.astype(BF16)
    wqi_t = w_qidx_up.T.astype(BF16)
    wkv = w_kv_up.reshape(KV_RANK, DSA_HEADS, 2, DSA_HEAD_DIM)
    wk = wkv[:, :, 0].reshape(KV_RANK, DSA_WIDTH).astype(BF16)
    wv_t = wkv[:, :, 1].reshape(KV_RANK, DSA_WIDTH).T.astype(BF16)
    full = lambda a: pl.BlockSpec(a.shape, lambda i, j: (0,) * a.ndim)
    qg = q_norm_g.reshape(1, Q_RANK)
    kvg = kv_norm_g.reshape(1, KV_RANK)
    lng = idx_ln_g.reshape(1, IDX_DIM)
    lnb = idx_ln_b.reshape(1, IDX_DIM)
    nqw = IDX_HEADS * LANES
    return pl.pallas_call(
        _dsa_prep_kernel,
        grid=(b, t // tm),
        in_specs=[pl.BlockSpec((1, tm, Q_RANK + KV_RANK), lambda i, j: (i, j, 0)),
                  pl.BlockSpec((1, tm, LANES), lambda i, j: (i, j, 0)),
                  full(qg), full(kvg), full(wq), full(wqi_t), full(wk), full(wv_t),
                  full(lng), full(lnb)],
        out_specs=[pl.BlockSpec((1, tm, DSA_WIDTH), lambda i, j: (i, j, 0)),
                   pl.BlockSpec((1, IDX_DIM, (tm // LANES) * nqw), lambda i, j: (i, 0, j)),
                   pl.BlockSpec((1, tm, DSA_WIDTH), lambda i, j: (i, j, 0)),
                   pl.BlockSpec((1, DSA_WIDTH, tm), lambda i, j: (i, 0, j)),
                   pl.BlockSpec((1, tm, IDX_DIM), lambda i, j: (i, j, 0)),
                   pl.BlockSpec((1, IDX_HEADS, tm), lambda i, j: (i, 0, j))],
        out_shape=[jax.ShapeDtypeStruct((b, t, DSA_WIDTH), BF16),
                   jax.ShapeDtypeStruct((b, IDX_DIM, (t // LANES) * nqw), BF16),
                   jax.ShapeDtypeStruct((b, t, DSA_WIDTH), BF16),
                   jax.ShapeDtypeStruct((b, DSA_WIDTH, t), BF16),
                   jax.ShapeDtypeStruct((b, t, IDX_DIM), BF16),
                   jax.ShapeDtypeStruct((b, IDX_HEADS, t), F32)],
        compiler_params=_params(("parallel", "parallel")),
    )(dsa3, small3, qg, kvg, wq, wqi_t, wk, wv_t, lng, lnb)


DSA_QB = 128
DSA_KC = 512
DSA_AC = DSA_KC
N_BIAS_TILES = 9


def _dsa_kernel(q_ref, qit_ref, wt_ref, ki_ref, k_ref, vt_ref, bias_ref, o_ref,
                keys_ref, *scratch, topk):
    acc_refs = scratch[:DSA_HEADS]
    s_refs = scratch[DSA_HEADS:DSA_HEADS + 2]
    p_refs = scratch[DSA_HEADS + 2:]
    i = pl.program_id(1)
    t0 = i * DSA_QB
    n_kc = (t0 + DSA_QB + DSA_KC - 1) // DSA_KC
    tq = t0 + lax.broadcasted_iota(I32, (1, DSA_QB), 1)
    qit = qit_ref[0]
    wt = wt_ref[0]

    def score_chunk(kc, carry):
        r0 = pl.multiple_of(kc * DSA_KC, DSA_KC)
        big = jnp.dot(ki_ref[0, pl.ds(r0, DSA_KC), :], qit, preferred_element_type=F32)
        sc = jnp.zeros((DSA_KC, DSA_QB), F32)
        for h in range(IDX_HEADS):
            sc = sc + jnp.maximum(big[:, h * DSA_QB:(h + 1) * DSA_QB], 0.0) * wt[h:h + 1]
        bits = pltpu.bitcast(sc, I32)
        key = bits ^ ((bits >> 31) & 0x7FFFFFFF)
        spos = r0 + lax.broadcasted_iota(I32, (DSA_KC, 1), 0)
        keys_ref[pl.ds(r0, DSA_KC), :] = jnp.where(spos <= tq, key, INT_MIN)
        return carry

    lax.fori_loop(0, n_kc, score_chunk, 0)

    def count_ge(cand):
        def body(kc, cnt):
            r0 = pl.multiple_of(kc * DSA_KC, DSA_KC)
            ge = (keys_ref[pl.ds(r0, DSA_KC), :] >= cand).astype(I32)
            return cnt + jnp.sum(ge.reshape(DSA_KC // SUBLANES, SUBLANES, DSA_QB), axis=0)
        cnt8 = lax.fori_loop(0, n_kc, body, jnp.zeros((SUBLANES, DSA_QB), I32))
        return jnp.sum(cnt8, axis=0, keepdims=True)

    zero = jnp.zeros((1, DSA_QB), I32)
    ans = jnp.where(count_ge(zero) >= topk, zero, jnp.full((1, DSA_QB), INT_MIN, I32))

    def bit_pass(b, ans):
        cand = ans | (jnp.int32(1) << (30 - b))
        return jnp.where(count_ge(cand) >= topk, cand, ans)

    ans = lax.fori_loop(0, 31, bit_pass, ans)
    thr = jnp.maximum(ans, INT_MIN + 1)

    for h in range(DSA_HEADS):
        acc_refs[h][...] = jnp.zeros((DSA_HEAD_DIM, DSA_QB), F32)
    for p_ref in p_refs:
        p_ref[...] = jnp.zeros(p_ref.shape, BF16)
    qb = q_ref[0]
    scale = DSA_HEAD_DIM ** -0.5
    sub = DSA_AC // DSA_QB

    def scores(kc, s_ref):
        r0 = pl.multiple_of(jnp.minimum(kc, n_kc - 1) * DSA_AC, DSA_AC)
        for h in range(DSA_HEADS):
            l0 = h * DSA_HEAD_DIM
            s_ref[h] = lax.dot_general(k_ref[0, pl.ds(r0, DSA_AC), l0:l0 + DSA_HEAD_DIM],
                                       qb[:, l0:l0 + DSA_HEAD_DIM], (((1,), (1,)), ((), ())),
                                       preferred_element_type=F32)

    def accumulate(kc, p_ref, alphas):
        r0 = pl.multiple_of(jnp.clip(kc, 0, n_kc - 1) * DSA_AC, DSA_AC)
        for h in range(DSA_HEADS):
            l0 = h * DSA_HEAD_DIM
            pv = jnp.dot(vt_ref[0, l0:l0 + DSA_HEAD_DIM, pl.ds(r0, DSA_AC)], p_ref[h],
                         preferred_element_type=F32)
            acc_refs[h][...] = acc_refs[h][...] * alphas[h] + pv

    def softmax(kc, s_ref, p_ref, ms, ls):
        r0 = pl.multiple_of(jnp.minimum(kc, n_kc - 1) * DSA_AC, DSA_AC)
        sel = (keys_ref[pl.ds(r0, DSA_AC), :] >= thr) & (kc < n_kc)
        tiles = [jnp.clip(i - (kc * sub + a), 0, N_BIAS_TILES - 1) for a in range(sub)]
        new_ms, new_ls, alphas = [], [], []
        for h in range(DSA_HEADS):
            bias = jnp.concatenate([bias_ref[h, bt] for bt in tiles], axis=0)
            logit = jnp.where(sel, s_ref[h] * scale + bias, -jnp.inf)
            m_new = jnp.maximum(ms[h], jnp.max(logit, axis=0, keepdims=True))
            m_safe = jnp.where(m_new == -jnp.inf, 0.0, m_new)
            alpha = jnp.exp(ms[h] - m_safe)
            p = jnp.exp(logit - m_safe)
            p_ref[h] = p.astype(BF16)
            new_ls.append(alpha * ls[h] + jnp.sum(p, axis=0, keepdims=True))
            new_ms.append(m_new)
            alphas.append(alpha)
        return tuple(new_ms), tuple(new_ls), tuple(alphas)

    def attend_pair(j, carry):
        ms, ls, alphas = carry
        for slot in range(2):
            kc = 2 * j + slot
            scores(kc + 1, s_refs[1 - slot])
            accumulate(kc - 1, p_refs[1 - slot], alphas)
            ms, ls, alphas = softmax(kc, s_refs[slot], p_refs[slot], ms, ls)
        return ms, ls, alphas

    scores(0, s_refs[0])
    m0 = tuple(jnp.full((1, DSA_QB), -jnp.inf, F32) for _ in range(DSA_HEADS))
    l0s = tuple(jnp.zeros((1, DSA_QB), F32) for _ in range(DSA_HEADS))
    one = tuple(jnp.ones((1, DSA_QB), F32) for _ in range(DSA_HEADS))
    _, ls, _ = lax.fori_loop(0, (n_kc + 2) // 2, attend_pair, (m0, l0s, one))
    o_ref[0] = jnp.concatenate([(acc_refs[h][...] / ls[h]).T for h in range(DSA_HEADS)], axis=1)


def _rel_bucket(dist):
    max_exact = REL_BUCKETS // 2
    n = jnp.maximum(dist, 0)
    nf = jnp.maximum(n, 1).astype(F32)
    large = max_exact + (jnp.log(nf / max_exact) / math.log(REL_MAX_DIST / max_exact)
                         * (REL_BUCKETS - max_exact)).astype(jnp.int32)
    large = jnp.minimum(large, REL_BUCKETS - 1)
    return jnp.where(n < max_exact, n, large)


def _bias_tiles(rel_bias):
    bt = DSA_QB
    span = 2 * bt
    dist = jnp.arange(-(bt - 1), N_BIAS_TILES * bt + 1)
    by_dist = rel_bias[_rel_bucket(dist)].astype(F32).T
    rows = jnp.stack([by_dist[:, j * bt:j * bt + span] for j in range(N_BIAS_TILES)], axis=1)
    rep = jnp.tile(rows, (1, 1, bt))[..., bt - 1:bt - 1 + bt * (span - 1)]
    return rep.reshape(DSA_HEADS, N_BIAS_TILES, bt, span - 1)[..., :bt]


def _dsa(q, qit, wt, ki, k, vt, bias, topk):
    b, t, _ = q.shape
    nqw = IDX_HEADS * DSA_QB
    t_pad = ((t + DSA_KC - 1) // DSA_KC) * DSA_KC
    kern = functools.partial(_dsa_kernel, topk=topk)
    return pl.pallas_call(
        kern,
        grid=(b, t // DSA_QB),
        in_specs=[pl.BlockSpec((1, DSA_QB, DSA_WIDTH), lambda i, j: (i, j, 0)),
                  pl.BlockSpec((1, IDX_DIM, nqw), lambda i, j: (i, 0, j)),
                  pl.BlockSpec((1, IDX_HEADS, DSA_QB), lambda i, j: (i, 0, j)),
                  pl.BlockSpec((1, t, IDX_DIM), lambda i, j: (i, 0, 0)),
                  pl.BlockSpec((1, t, DSA_WIDTH), lambda i, j: (i, 0, 0)),
                  pl.BlockSpec((1, DSA_WIDTH, t), lambda i, j: (i, 0, 0)),
                  pl.BlockSpec(bias.shape, lambda i, j: (0, 0, 0, 0))],
        out_specs=pl.BlockSpec((1, DSA_QB, DSA_WIDTH), lambda i, j: (i, j, 0)),
        out_shape=jax.ShapeDtypeStruct((b, t, DSA_WIDTH), F32),
        scratch_shapes=[pltpu.VMEM((t_pad, DSA_QB), I32)]
        + [pltpu.VMEM((DSA_HEAD_DIM, DSA_QB), F32)] * DSA_HEADS
        + [pltpu.VMEM((DSA_HEADS, DSA_AC, DSA_QB), F32)] * 2
        + [pltpu.VMEM((DSA_HEADS, DSA_AC, DSA_QB), BF16)] * 2,
        compiler_params=_params(("parallel", "arbitrary")),
    )(q, qit, wt, ki, k, vt, bias)


def _top16_rows(s, n_rows):
    tm = s.shape[1]
    rid = lax.broadcasted_iota(I32, (n_rows, tm), 0)
    vals, idxs = [], []
    for _ in range(PEER_TOPK):
        m = jnp.max(s, axis=0, keepdims=True)
        ix = jnp.min(jnp.where(s == m, rid, n_rows), axis=0, keepdims=True)
        vals.append(m)
        idxs.append(ix)
        s = jnp.where(rid == ix, -jnp.inf, s)
    return jnp.concatenate(vals, axis=0), jnp.concatenate(idxs, axis=0)


def _mix_kernel(x_ref, oa_ref, ob_ref, wo_ref, g2_ref, wq_ref, sk1_ref, sk2_ref,
                x1_ref, h2_ref, eidx_ref, gates_ref):
    tm = x_ref.shape[0]
    o = jnp.concatenate([oa_ref[...], ob_ref[...]], axis=1)
    x1 = x_ref[...] + _bdot(o, wo_ref[...])
    x1_ref[...] = x1
    h2 = _rms(x1, g2_ref[...])
    h2_ref[...] = h2
    query = _bdot(h2, wq_ref[...])
    half = PEER_KEY_DIM // 2
    groups = [(0, PEER_TOPK)] + [(a, PEER_TOPK // (a + 1)) for a in range(1, SUBLANES)]
    n_cand = PEER_TOPK + SUBLANES * SUBLANES
    pid = lax.broadcasted_iota(I32, (n_cand, tm), 0)
    row8 = lax.broadcasted_iota(I32, (SUBLANES, tm), 0)
    e_rows, g_rows = [], []
    for h in range(PEER_HEADS):
        q1 = query[:, h * PEER_KEY_DIM:h * PEER_KEY_DIM + half]
        q2 = query[:, h * PEER_KEY_DIM + half:(h + 1) * PEER_KEY_DIM]
        s1 = _bdot_nt(sk1_ref[h], q1)
        s2 = _bdot_nt(sk2_ref[h], q2)
        v1, i1 = _top16_rows(s1, PEER_KEYS)
        v2, i2 = _top16_rows(s2, PEER_KEYS)
        cands, cidxs = [], []
        for a, nb in groups:
            rows = PEER_TOPK if a == 0 else SUBLANES
            cv = v1[a:a + 1] + v2[:rows]
            ci = i1[a:a + 1] * PEER_KEYS + i2[:rows]
            if nb < rows:
                cv = jnp.where(row8 < nb, cv, -jnp.inf)
            cands.append(cv)
            cidxs.append(ci)
        cands.append(v1[SUBLANES:] + v2[0:1])
        cidxs.append(i1[SUBLANES:] * PEER_KEYS + i2[0:1])
        cand = jnp.concatenate(cands, axis=0)
        cidx = jnp.concatenate(cidxs, axis=0)
        tops, eids = [], []
        for _ in range(PEER_TOPK):
            m = jnp.max(cand, axis=0, keepdims=True)
            pos = jnp.min(jnp.where(cand == m, pid, n_cand), axis=0, keepdims=True)
            hit = pid == pos
            eids.append(jnp.max(jnp.where(hit, cidx, -1), axis=0, keepdims=True))
            tops.append(m)
            cand = jnp.where(hit, -jnp.inf, cand)
        top_s = jnp.concatenate(tops, axis=0)
        ex = jnp.exp(top_s - top_s[0:1])
        g_rows.append(ex / jnp.sum(ex, axis=0, keepdims=True))
        e_rows.append(jnp.concatenate(eids, axis=0))
    gates_ref[...] = jnp.concatenate(g_rows, axis=0).T
    eidx_ref[...] = (jnp.concatenate(e_rows, axis=0) * PEER_SLAB).T


def _mix(x2, o_a, o_b, w_out, ln2_g, w_query, sk1, sk2, tm):
    n = x2.shape[0]
    wo = w_out.astype(BF16)
    wq = w_query.astype(BF16)
    sk1 = sk1.astype(BF16)
    sk2 = sk2.astype(BF16)
    nsel = PEER_HEADS * PEER_TOPK
    row = lambda w: pl.BlockSpec((tm, w), lambda i: (i, 0))
    full = lambda a: pl.BlockSpec(a.shape, lambda i: (0,) * a.ndim)
    g2 = ln2_g.reshape(1, D_MODEL)
    return pl.pallas_call(
        _mix_kernel,
        grid=(n // tm,),
        in_specs=[row(D_MODEL), row(GDN_WIDTH), row(DSA_WIDTH), full(wo), full(g2), full(wq),
                  full(sk1), full(sk2)],
        out_specs=[row(D_MODEL), row(D_MODEL), row(nsel), row(nsel)],
        out_shape=[jax.ShapeDtypeStruct((n, D_MODEL), F32),
                   jax.ShapeDtypeStruct((n, D_MODEL), F32),
                   jax.ShapeDtypeStruct((n, nsel), I32),
                   jax.ShapeDtypeStruct((n, nsel), F32)],
        compiler_params=_params(("parallel",)),
    )(x2, o_a, o_b, wo, g2, wq, sk1, sk2)


PEER_TB = 32
PEER_SLAB = 4
PEER_HALF = D_MODEL // 2
PEER_NSEL = PEER_HEADS * PEER_TOPK
PEER_COLS = PEER_NSEL * PEER_SLAB


def _pack_table(w):
    e = w.shape[0]
    bits = lax.bitcast_convert_type(w.astype(jnp.bfloat16), jnp.uint16).astype(jnp.uint32)
    words = bits[:, :PEER_HALF] | (bits[:, PEER_HALF:] << 16)
    return lax.bitcast_convert_type(words, I32).reshape(e * PEER_SLAB, LANES)


def _unpack_words(words):
    lo = lax.bitcast_convert_type(words << 16, F32)
    hi = lax.bitcast_convert_type(words & jnp.int32(-65536), F32)
    return lo, hi


def _split_bf16(x):
    head = x.astype(jnp.bfloat16).astype(F32)
    return head, x - head


def _gather_slabs(idx_ref, row, tab_ref):
    slabs = []
    for k in range(PEER_NSEL):
        start = pl.multiple_of(idx_ref[row, k], PEER_SLAB)
        slabs.append(tab_ref[pl.ds(start, PEER_SLAB), :])
    return jnp.concatenate(slabs, axis=0)


def _load_table(tab_hbm, tab_ref, sem):
    @pl.when(pl.program_id(0) == 0)
    def _():
        cp = pltpu.make_async_copy(tab_hbm, tab_ref, sem)
        cp.start()
        cp.wait()


def _index_copy(idx_hbm, block, idx_ref, sem):
    return pltpu.make_async_copy(idx_hbm.at[pl.ds(block * PEER_TB, PEER_TB)], idx_ref, sem)


def _peer_blocks(idx_hbm, idx_a, idx_b, sems, process):
    i = pl.program_id(0)
    last = pl.num_programs(0) - 1

    @pl.when(i == 0)
    def _():
        _index_copy(idx_hbm, 0, idx_a, sems.at[0]).start()

    _index_copy(idx_hbm, 2 * i + 1, idx_b, sems.at[1]).start()
    _index_copy(idx_hbm, 2 * i, idx_a, sems.at[0]).wait()
    process(0, idx_a)

    @pl.when(i < last)
    def _():
        _index_copy(idx_hbm, 2 * i + 2, idx_a, sems.at[0]).start()

    _index_copy(idx_hbm, 2 * i + 1, idx_b, sems.at[1]).wait()
    process(1, idx_b)


def _slab_mask():
    j = lax.broadcasted_iota(I32, (SUBLANES, PEER_COLS), 0)
    col = lax.broadcasted_iota(I32, (SUBLANES, PEER_COLS), 1)
    return (col % PEER_SLAB) == (j % PEER_SLAB)


def _peer_u_kernel(x3_ref, gates_ref, rep_ref, idx_hbm, tab_hbm, c_ref,
                   tab_ref, idx_a, idx_b, z_ref, sem, isems):
    _load_table(tab_hbm, tab_ref, sem)
    mask = _slab_mask()
    lane = lax.broadcasted_iota(I32, (SUBLANES, PEER_COLS), 1)

    def process(half, idx_ref):
        for t in range(PEER_TB):
            tok = half * PEER_TB + t
            xh, xt = _split_bf16(x3_ref[tok])
            a_lo = jnp.concatenate([xh[:PEER_SLAB], xt[:PEER_SLAB]], axis=0)
            a_hi = jnp.concatenate([xh[PEER_SLAB:], xt[PEER_SLAB:]], axis=0)
            lo, hi = _unpack_words(_gather_slabs(idx_ref, t, tab_ref))
            r = _bdot_nt(a_lo, lo) + _bdot_nt(a_hi, hi)
            z_ref[tok:tok + 1, :] = jnp.sum(jnp.where(mask, r, 0.0), axis=0, keepdims=True)

    _peer_blocks(idx_hbm, idx_a, idx_b, isems, process)

    gh, gt = _split_bf16(gates_ref[...])
    rep = rep_ref[...]
    gates_rep = _bdot(gh, rep) + _bdot(gt, rep)
    for g in range(2 * PEER_TB // SUBLANES):
        z = z_ref[g * SUBLANES:(g + 1) * SUBLANES, :]
        s = 1
        while s < PEER_SLAB:
            up = pltpu.roll(z, PEER_COLS - s, axis=1)
            dn = pltpu.roll(z, s, axis=1)
            z = z + jnp.where((lane // s) % 2 == 0, up, dn)
            s *= 2
        gelu = 0.5 * z * (1.0 + lax.erf(z * (2.0 ** -0.5)))
        c_ref[g * SUBLANES:(g + 1) * SUBLANES, :] = gelu * gates_rep[g * SUBLANES:(g + 1) * SUBLANES]


def _peer_v_kernel(c_in_ref, x1_ref, fg_ref, idx_hbm, tab_hbm, o_ref,
                   tab_ref, idx_a, idx_b, sem, isems):
    _load_table(tab_hbm, tab_ref, sem)
    mask = _slab_mask()
    rows = lax.broadcasted_iota(I32, (SUBLANES, PEER_COLS), 0)

    def process(half, idx_ref):
        for t in range(PEER_TB):
            tok = half * PEER_TB + t
            ch, ct = _split_bf16(c_in_ref[tok:tok + 1, :])
            c8 = jnp.where(mask, jnp.where(rows < PEER_SLAB, ch, ct), 0.0)
            lo, hi = _unpack_words(_gather_slabs(idx_ref, t, tab_ref))
            y_lo = _bdot(c8, lo)
            y_hi = _bdot(c8, hi)
            o_ref[tok] = jnp.concatenate([y_lo[:PEER_SLAB] + y_lo[PEER_SLAB:],
                                          y_hi[:PEER_SLAB] + y_hi[PEER_SLAB:]], axis=0)

    _peer_blocks(idx_hbm, idx_a, idx_b, isems, process)
    zt = x1_ref[...] + o_ref[...]
    ms = jnp.sum(jnp.sum(zt * zt, axis=2, keepdims=True), axis=1, keepdims=True) * (1.0 / D_MODEL)
    o_ref[...] = zt * lax.rsqrt(ms + EPS) * fg_ref[...]


def _peer(slab_row, x1, h2, gates, final_g, peer_u, peer_v):
    n = x1.shape[0]
    step = 2 * PEER_TB
    u_tab = _pack_table(peer_u)
    v_tab = _pack_table(peer_v)
    x3 = h2.reshape(n, SUBLANES, LANES)
    x1_3 = x1.reshape(n, SUBLANES, LANES)
    fg3 = final_g.reshape(SUBLANES, LANES)
    rep = (jnp.arange(PEER_COLS)[None, :] // PEER_SLAB == jnp.arange(PEER_NSEL)[:, None]).astype(BF16)
    tok3 = pl.BlockSpec((step, SUBLANES, LANES), lambda i: (i, 0, 0))
    sel = pl.BlockSpec((step, PEER_NSEL), lambda i: (i, 0))
    wide = pl.BlockSpec((step, PEER_COLS), lambda i: (i, 0))
    full = lambda a: pl.BlockSpec(a.shape, lambda i: (0,) * a.ndim)
    hbm = pl.BlockSpec(memory_space=pl.ANY)
    idx_buf = pltpu.SMEM((PEER_TB, PEER_NSEL), I32)
    coef = pl.pallas_call(
        _peer_u_kernel,
        grid=(n // step,),
        in_specs=[tok3, sel, full(rep), hbm, hbm],
        out_specs=wide,
        out_shape=jax.ShapeDtypeStruct((n, PEER_COLS), F32),
        scratch_shapes=[pltpu.VMEM(u_tab.shape, I32), idx_buf, idx_buf,
                        pltpu.VMEM((step, PEER_COLS), F32),
                        pltpu.SemaphoreType.DMA(()), pltpu.SemaphoreType.DMA((2,))],
        compiler_params=_params(("arbitrary",)),
    )(x3, gates, rep, slab_row, u_tab)
    out3 = pl.pallas_call(
        _peer_v_kernel,
        grid=(n // step,),
        in_specs=[wide, tok3, full(fg3), hbm, hbm],
        out_specs=tok3,
        out_shape=jax.ShapeDtypeStruct((n, SUBLANES, LANES), F32),
        scratch_shapes=[pltpu.VMEM(v_tab.shape, I32), idx_buf, idx_buf,
                        pltpu.SemaphoreType.DMA(()), pltpu.SemaphoreType.DMA((2,))],
        compiler_params=_params(("arbitrary",)),
    )(coef, x1_3, fg3, slab_row, v_tab)
    return out3.reshape(n, D_MODEL)


def _permute_w_in(w_in):
    o = [0]
    for s in (GDN_WIDTH, GDN_WIDTH, GDN_WIDTH, GDN_WIDTH, GDN_HEADS, GDN_HEADS, Q_RANK, KV_RANK,
              IDX_DIM, IDX_HEADS):
        o.append(o[-1] + s)
    gq_gz = w_in[:, o[0]:o[4]]
    ga = w_in[:, o[4]:o[5]]
    gb = w_in[:, o[5]:o[6]]
    cq_ckv = w_in[:, o[6]:o[8]]
    kidx = w_in[:, o[8]:o[9]]
    widx = w_in[:, o[9]:o[10]]
    pad = jnp.zeros((w_in.shape[0], LANES - (IDX_DIM + IDX_HEADS + 2 * GDN_HEADS)), w_in.dtype)
    return jnp.concatenate([gq_gz, cq_ckv, kidx, widx, ga, gb, pad], axis=1).astype(BF16)


def kernel(x, ln1_g, w_in, conv_w, a_log, dt_bias, gdn_norm_g, q_norm_g, kv_norm_g, w_q_up,
           w_qidx_up, w_kv_up, idx_ln_g, idx_ln_b, w_out, ln2_g, peer_w_query, peer_sub_keys_1,
           peer_sub_keys_2, peer_u, peer_v, rel_bias, final_g):
    b, t, d = x.shape
    n = b * t
    assert w_in.shape[0] == 1, "single-layer block only"
    l = 0
    topk = min(IDX_TOPK_MAX, t // 4)
    bias = _bias_tiles(rel_bias)
    xc = x.reshape(n, d)
    gdn, dsa, small = _inproj(xc, ln1_g[l], _permute_w_in(w_in[l]), tm=256)
    gdn3 = gdn.reshape(b, t, -1)
    dsa3 = dsa.reshape(b, t, -1)
    small3 = small.reshape(b, t, LANES)
    o_a = _gdn(gdn3, small3, conv_w[l], a_log[l], dt_bias[l], gdn_norm_g[l], tb=256)
    q, qit, k, vt, ki, wt = _dsa_prep(dsa3, small3, q_norm_g[l], kv_norm_g[l], w_q_up[l],
                                      w_qidx_up[l], w_kv_up[l], idx_ln_g[l], idx_ln_b[l], tm=512)
    o_b = _dsa(q, qit, wt, ki, k, vt, bias, topk)
    x1, h2, slab_row, gates = _mix(xc, o_a.reshape(n, -1), o_b.reshape(n, -1), w_out[l], ln2_g[l],
                                   peer_w_query[l], peer_sub_keys_1[l], peer_sub_keys_2[l], tm=256)
    out = _peer(slab_row, x1, h2, gates, final_g, peer_u[l], peer_v[l])
    return out.reshape(b, t, d)
```

```python
import functools
import math

import jax
import jax.numpy as jnp
from jax import lax
from jax.experimental import pallas as pl
from jax.experimental.pallas import tpu as pltpu

F32 = jnp.float32
BF16 = jnp.bfloat16
I32 = jnp.int32

D_MODEL = 1024
GDN_HEADS = 4
GDN_HEAD_DIM = 128
GDN_WIDTH = GDN_HEADS * GDN_HEAD_DIM
CONV_WIDTH = 4
GDN_CHUNK = 64
DSA_HEADS = 4
DSA_HEAD_DIM = 128
DSA_WIDTH = DSA_HEADS * DSA_HEAD_DIM
Q_RANK = 256
KV_RANK = 256
IDX_HEADS = 16
IDX_DIM = 64
IDX_TOPK_MAX = 256
REL_BUCKETS = 32
REL_MAX_DIST = 1024
PEER_HEADS = 8
PEER_KEYS = 128
PEER_KEY_DIM = 256
PEER_TOPK = 16
EPS = 1e-6

LANES = 128
SUBLANES = 8
VMEM_LIMIT = 56 * 1024 * 1024

SM_KIDX = 0
SM_WIDX = 64
SM_GA = 80
SM_GB = 84
INT_MIN = -2 ** 31
LOG2E = math.log2(math.e)


def _bdot(a, b):
    return jnp.dot(a.astype(BF16), b.astype(BF16), preferred_element_type=F32)


def _bdot_nt(a, b):
    return lax.dot_general(a.astype(BF16), b.astype(BF16), (((1,), (1,)), ((), ())),
                           preferred_element_type=F32)


def _bdot_tn(a, b):
    return lax.dot_general(a.astype(BF16), b.astype(BF16), (((0,), (0,)), ((), ())),
                           preferred_element_type=F32)


def _hdot(a, b):
    ah = a.astype(jnp.bfloat16)
    bh = b.astype(jnp.bfloat16)
    al = (a - ah.astype(F32)).astype(jnp.bfloat16)
    bl = (b - bh.astype(F32)).astype(jnp.bfloat16)
    dot = functools.partial(jnp.dot, preferred_element_type=F32)
    return dot(ah, bh) + (dot(ah, bl) + dot(al, bh))


def _rms(x, g):
    return x * lax.rsqrt(jnp.mean(x * x, axis=-1, keepdims=True) + EPS) * g


def _silu(x):
    return x * (1.0 / (1.0 + jnp.exp(-x)))


def _chunk_loop(n, unroll, body, carry):
    def group(j, c):
        for u in range(unroll):
            c = body(unroll * j + u, c)
        return c

    carry = lax.fori_loop(0, n // unroll, group, carry)
    return lax.fori_loop((n // unroll) * unroll, n, body, carry)


def _params(sem):
    return pltpu.CompilerParams(dimension_semantics=sem, vmem_limit_bytes=VMEM_LIMIT)


def _inproj_kernel(x_ref, g_ref, w_ref, gdn_ref, dsa_ref, small_ref):
    h = _rms(x_ref[...], g_ref[...])
    p = _bdot(h, w_ref[...])
    n_gdn = gdn_ref.shape[-1]
    n_dsa = dsa_ref.shape[-1]
    gdn_ref[...] = p[:, :n_gdn]
    dsa_ref[...] = p[:, n_gdn:n_gdn + n_dsa]
    small_ref[...] = p[:, n_gdn + n_dsa:]


def _inproj(x2, ln1_g, w_perm, tm):
    n = x2.shape[0]
    n_gdn = 4 * GDN_WIDTH
    n_dsa = Q_RANK + KV_RANK
    return pl.pallas_call(
        _inproj_kernel,
        grid=(n // tm,),
        in_specs=[pl.BlockSpec((tm, D_MODEL), lambda i: (i, 0)),
                  pl.BlockSpec((1, D_MODEL), lambda i: (0, 0)),
                  pl.BlockSpec(w_perm.shape, lambda i: (0, 0))],
        out_specs=[pl.BlockSpec((tm, n_gdn), lambda i: (i, 0)),
                   pl.BlockSpec((tm, n_dsa), lambda i: (i, 0)),
                   pl.BlockSpec((tm, LANES), lambda i: (i, 0))],
        out_shape=[jax.ShapeDtypeStruct((n, n_gdn), F32),
                   jax.ShapeDtypeStruct((n, n_dsa), F32),
                   jax.ShapeDtypeStruct((n, LANES), F32)],
        compiler_params=_params(("parallel",)),
    )(x2, ln1_g.reshape(1, D_MODEL), w_perm)


def _gdn_kernel(gdn_ref, small_ref, convw_ref, alog_ref, dtb_ref, ng_ref, o_ref,
                tail_ref, state_ref):
    tb = gdn_ref.shape[1]
    c = GDN_CHUNK
    n_chunks = tb // c
    w3 = 3 * GDN_WIDTH

    @pl.when(pl.program_id(1) == 0)
    def _():
        tail_ref[...] = jnp.zeros_like(tail_ref)
        state_ref[...] = jnp.zeros_like(state_ref)

    blk = gdn_ref[0]
    xin = blk[:, :w3]
    z = blk[:, w3:]
    tail = tail_ref[...]
    cw = convw_ref[...]
    acc = xin * cw[CONV_WIDTH - 1:CONV_WIDTH]
    rows8 = lax.broadcasted_iota(I32, (SUBLANES, w3), 0)
    for k in range(1, CONV_WIDTH):
        xk = pltpu.roll(xin, k, axis=0)
        fix = pltpu.roll(tail, k, axis=0)
        top = jnp.where(rows8 < k, fix, xk[:SUBLANES])
        xk = jnp.concatenate([top, xk[SUBLANES:]], axis=0)
        acc = acc + xk * cw[CONV_WIDTH - 1 - k:CONV_WIDTH - k]
    tail_ref[...] = xin[tb - SUBLANES:]
    qkv = _silu(acc)

    sm = small_ref[0]
    sp = sm + dtb_ref[...]
    softplus = jnp.maximum(sp, 0.0) + jnp.log(1.0 + jnp.exp(-jnp.abs(sp)))
    gl = -jnp.exp(alog_ref[...]) * softplus
    beta = 1.0 / (1.0 + jnp.exp(-sm))
    rin = lax.broadcasted_iota(I32, (tb, LANES), 0) % c
    s = 1
    while s < c:
        gl = gl + jnp.where(rin >= s, pltpu.roll(gl, s, axis=0), 0.0)
        s *= 2
    g_t = gl.T
    eg = jnp.exp(gl)

    ri = lax.broadcasted_iota(I32, (c, c), 0)
    ci = lax.broadcasted_iota(I32, (c, c), 1)
    tril = ri >= ci
    strict = ri > ci
    eye = (ri == ci).astype(F32)
    ng = ng_ref[...]
    scale = GDN_HEAD_DIM ** -0.5

    probs = [(ch, h) for ch in range(n_chunks) for h in range(GDN_HEADS)]
    qs, ks, gcols, egcols, decays, kbs, vbs, ms, t_invs = ({} for _ in range(9))
    for ch, h in probs:
        r0, l0 = ch * c, h * GDN_HEAD_DIM
        qh = qkv[r0:r0 + c, l0:l0 + GDN_HEAD_DIM]
        kh = qkv[r0:r0 + c, GDN_WIDTH + l0:GDN_WIDTH + l0 + GDN_HEAD_DIM]
        vh = qkv[r0:r0 + c, 2 * GDN_WIDTH + l0:2 * GDN_WIDTH + l0 + GDN_HEAD_DIM]
        qs[ch, h] = qh * lax.rsqrt(jnp.sum(qh * qh, axis=-1, keepdims=True) + EPS) * scale
        kh = kh * lax.rsqrt(jnp.sum(kh * kh, axis=-1, keepdims=True) + EPS)
        ks[ch, h] = kh
        gcol = gl[r0:r0 + c, SM_GA + h:SM_GA + h + 1]
        grow = g_t[SM_GA + h:SM_GA + h + 1, r0:r0 + c]
        gcols[ch, h] = gcol
        egcols[ch, h] = eg[r0:r0 + c, SM_GA + h:SM_GA + h + 1]
        bcol = beta[r0:r0 + c, SM_GB + h:SM_GB + h + 1]
        decays[ch, h] = jnp.exp(jnp.where(tril, gcol - grow, -jnp.inf))
        kbs[ch, h] = kh * bcol
        vbs[ch, h] = vh * bcol
    for p in probs:
        a_mat = jnp.where(strict, _bdot_nt(kbs[p], ks[p]) * decays[p], 0.0)
        ms[p] = -a_mat
        t_invs[p] = eye + ms[p]
    step = 1
    while step < c // 2:
        for p in probs:
            ms[p] = _hdot(ms[p], ms[p])
        for p in probs:
            t_invs[p] = t_invs[p] + _hdot(t_invs[p], ms[p])
        step *= 2
    us = {p: _bdot(t_invs[p], vbs[p]) for p in probs}
    ws = {p: _bdot(t_invs[p], kbs[p] * egcols[p]) for p in probs}
    attns = {p: _bdot_nt(qs[p], ks[p]) * decays[p] for p in probs}

    out_rows = []
    for ch in range(n_chunks):
        r0 = ch * c
        out_heads = []
        for h in range(GDN_HEADS):
            p = (ch, h)
            l0 = h * GDN_HEAD_DIM
            st = state_ref[h]
            v_new = us[p] - _bdot(ws[p], st)
            o = _bdot(qs[p] * egcols[p], st) + _bdot(attns[p], v_new)
            glast = gcols[p][c - 1:c]
            state_ref[h] = st * jnp.exp(glast) + _bdot_tn(ks[p] * jnp.exp(glast - gcols[p]), v_new)
            zh = z[r0:r0 + c, l0:l0 + GDN_HEAD_DIM]
            out_heads.append(_rms(o, ng) * _silu(zh))
        out_rows.append(jnp.concatenate(out_heads, axis=1))
    o_ref[0] = jnp.concatenate(out_rows, axis=0)


def _gdn(gdn3, small3, conv_w, a_log, dt_bias, norm_g, tb):
    b, t, _ = gdn3.shape
    alog_pad = jnp.zeros((1, LANES), F32).at[0, SM_GA:SM_GA + GDN_HEADS].set(a_log)
    dtb_pad = jnp.zeros((1, LANES), F32).at[0, SM_GA:SM_GA + GDN_HEADS].set(dt_bias)
    return pl.pallas_call(
        _gdn_kernel,
        grid=(b, t // tb),
        in_specs=[pl.BlockSpec((1, tb, 4 * GDN_WIDTH), lambda i, j: (i, j, 0)),
                  pl.BlockSpec((1, tb, LANES), lambda i, j: (i, j, 0)),
                  pl.BlockSpec((CONV_WIDTH, 3 * GDN_WIDTH), lambda i, j: (0, 0)),
                  pl.BlockSpec((1, LANES), lambda i, j: (0, 0)),
                  pl.BlockSpec((1, LANES), lambda i, j: (0, 0)),
                  pl.BlockSpec((1, GDN_HEAD_DIM), lambda i, j: (0, 0))],
        out_specs=pl.BlockSpec((1, tb, GDN_WIDTH), lambda i, j: (i, j, 0)),
        out_shape=jax.ShapeDtypeStruct((b, t, GDN_WIDTH), F32),
        scratch_shapes=[pltpu.VMEM((SUBLANES, 3 * GDN_WIDTH), F32),
                        pltpu.VMEM((GDN_HEADS, GDN_HEAD_DIM, GDN_HEAD_DIM), F32)],
        compiler_params=_params(("parallel", "arbitrary")),
    )(gdn3, small3, conv_w, alog_pad, dtb_pad, norm_g.reshape(1, GDN_HEAD_DIM))


def _dsa_prep_kernel(dsa_ref, small_ref, qg_ref, kvg_ref, wq_ref, wqi_t_ref, wk_ref, wv_t_ref,
                     lng_ref, lnb_ref, q_ref, qit_ref, k_ref, vt_ref, ki_ref, wt_ref):
    tm = dsa_ref.shape[1]
    blk = dsa_ref[0]
    cq = _rms(blk[:, :Q_RANK], qg_ref[...])
    ckv = _rms(blk[:, Q_RANK:], kvg_ref[...])
    q_ref[0] = _bdot(cq, wq_ref[...]).astype(BF16)
    k_ref[0] = _bdot(ckv, wk_ref[...]).astype(BF16)
    vt_ref[0] = _bdot_nt(wv_t_ref[...], ckv).astype(BF16)
    qit = _bdot_nt(wqi_t_ref[...], cq)
    nqb = tm // LANES
    cols = []
    for j in range(nqb):
        for h in range(IDX_HEADS):
            cols.append(qit[h * IDX_DIM:(h + 1) * IDX_DIM, j * LANES:(j + 1) * LANES])
    qit_ref[0] = jnp.concatenate(cols, axis=1).astype(BF16)
    sm = small_ref[0]
    kx = sm[:, SM_KIDX:SM_KIDX + IDX_DIM]
    mu = jnp.mean(kx, axis=-1, keepdims=True)
    xc = kx - mu
    kn = xc * lax.rsqrt(jnp.mean(xc * xc, axis=-1, keepdims=True) + EPS)
    ki_ref[0] = (kn * lng_ref[...] + lnb_ref[...]).astype(BF16)
    sm_t = sm.T
    wt_ref[0] = sm_t[SM_WIDX:SM_WIDX + IDX_HEADS] * (IDX_HEADS ** -0.5 * IDX_DIM ** -0.5)


def _dsa_prep(dsa3, small3, q_norm_g, kv_norm_g, w_q_up, w_qidx_up, w_kv_up, idx_ln_g, idx_ln_b, tm):
    b, t, _ = dsa3.shape
    wq = w_q_up.astype(BF16)
    wqi_t = w_qidx_up.T.astype(BF16)
    wkv = w_kv_up.reshape(KV_RANK, DSA_HEADS, 2, DSA_HEAD_DIM)
    wk = wkv[:, :, 0].reshape(KV_RANK, DSA_WIDTH).astype(BF16)
    wv_t = wkv[:, :, 1].reshape(KV_RANK, DSA_WIDTH).T.astype(BF16)
    full = lambda a: pl.BlockSpec(a.shape, lambda i, j: (0,) * a.ndim)
    qg = q_norm_g.reshape(1, Q_RANK)
    kvg = kv_norm_g.reshape(1, KV_RANK)
    lng = idx_ln_g.reshape(1, IDX_DIM)
    lnb = idx_ln_b.reshape(1, IDX_DIM)
    nqw = IDX_HEADS * LANES
    return pl.pallas_call(
        _dsa_prep_kernel,
        grid=(b, t // tm),
        in_specs=[pl.BlockSpec((1, tm, Q_RANK + KV_RANK), lambda i, j: (i, j, 0)),
                  pl.BlockSpec((1, tm, LANES), lambda i, j: (i, j, 0)),
                  full(qg), full(kvg), full(wq), full(wqi_t), full(wk), full(wv_t),
                  full(lng), full(lnb)],
        out_specs=[pl.BlockSpec((1, tm, DSA_WIDTH), lambda i, j: (i, j, 0)),
                   pl.BlockSpec((1, IDX_DIM, (tm // LANES) * nqw), lambda i, j: (i, 0, j)),
                   pl.BlockSpec((1, tm, DSA_WIDTH), lambda i, j: (i, j, 0)),
                   pl.BlockSpec((1, DSA_WIDTH, tm), lambda i, j: (i, 0, j)),
                   pl.BlockSpec((1, tm, IDX_DIM), lambda i, j: (i, j, 0)),
                   pl.BlockSpec((1, IDX_HEADS, tm), lambda i, j: (i, 0, j))],
        out_shape=[jax.ShapeDtypeStruct((b, t, DSA_WIDTH), BF16),
                   jax.ShapeDtypeStruct((b, IDX_DIM, (t // LANES) * nqw), BF16),
                   jax.ShapeDtypeStruct((b, t, DSA_WIDTH), BF16),
                   jax.ShapeDtypeStruct((b, DSA_WIDTH, t), BF16),
                   jax.ShapeDtypeStruct((b, t, IDX_DIM), BF16),
                   jax.ShapeDtypeStruct((b, IDX_HEADS, t), F32)],
        compiler_params=_params(("parallel", "parallel")),
    )(dsa3, small3, qg, kvg, wq, wqi_t, wk, wv_t, lng, lnb)


DSA_QB = 128
DSA_KC = 512
DSA_AC = DSA_KC
N_BIAS_TILES = 9


def _dsa_kernel(q_ref, qit_ref, wt_ref, ki_ref, k_ref, vt_ref, bias_ref, o_ref,
                keys_ref, *acc_refs, topk):
    i = pl.program_id(1)
    t0 = i * DSA_QB
    n_kc = (t0 + DSA_QB + DSA_KC - 1) // DSA_KC
    tq = t0 + lax.broadcasted_iota(I32, (1, DSA_QB), 1)
    qit = qit_ref[0]
    wt = wt_ref[0]

    def score_chunk(kc, carry):
        r0 = pl.multiple_of(kc * DSA_KC, DSA_KC)
        big = jnp.dot(ki_ref[0, pl.ds(r0, DSA_KC), :], qit, preferred_element_type=F32)
        sc = jnp.zeros((DSA_KC, DSA_QB), F32)
        for h in range(IDX_HEADS):
            sc = sc + jnp.maximum(big[:, h * DSA_QB:(h + 1) * DSA_QB], 0.0) * wt[h:h + 1]
        bits = pltpu.bitcast(sc, I32)
        key = bits ^ ((bits >> 31) & 0x7FFFFFFF)
        spos = r0 + lax.broadcasted_iota(I32, (DSA_KC, 1), 0)
        keys_ref[pl.ds(r0, DSA_KC), :] = jnp.where(spos <= tq, key, INT_MIN)
        return carry

    _chunk_loop(n_kc, 2, score_chunk, 0)

    def count_ge(cand):
        def body(kc, cnt):
            r0 = pl.multiple_of(kc * DSA_KC, DSA_KC)
            ge = (keys_ref[pl.ds(r0, DSA_KC), :] >= cand).astype(I32)
            return cnt + jnp.sum(ge.reshape(DSA_KC // SUBLANES, SUBLANES, DSA_QB), axis=0)
        cnt8 = lax.fori_loop(0, n_kc, body, jnp.zeros((SUBLANES, DSA_QB), I32))
        return jnp.sum(cnt8, axis=0, keepdims=True)

    zero = jnp.zeros((1, DSA_QB), I32)
    ans = jnp.where(count_ge(zero) >= topk, zero, jnp.full((1, DSA_QB), INT_MIN, I32))

    def bit_pass(b, ans):
        cand = ans | (jnp.int32(1) << (30 - b))
        return jnp.where(count_ge(cand) >= topk, cand, ans)

    ans = lax.fori_loop(0, 31, bit_pass, ans)
    thr = jnp.maximum(ans, INT_MIN + 1)

    for h in range(DSA_HEADS):
        acc_refs[h][...] = jnp.zeros((DSA_HEAD_DIM, DSA_QB), F32)
    qb = q_ref[0]
    scale = DSA_HEAD_DIM ** -0.5 * LOG2E
    sub = DSA_AC // DSA_QB

    def attend(kc, carry):
        ms, ls = carry
        r0 = pl.multiple_of(kc * DSA_AC, DSA_AC)
        sel = keys_ref[pl.ds(r0, DSA_AC), :] >= thr
        tiles = [jnp.clip(i - (kc * sub + a), 0, N_BIAS_TILES - 1) for a in range(sub)]
        new_ms, new_ls = [], []
        for h in range(DSA_HEADS):
            l0 = h * DSA_HEAD_DIM
            s_t = lax.dot_general(k_ref[0, pl.ds(r0, DSA_AC), l0:l0 + DSA_HEAD_DIM],
                                  qb[:, l0:l0 + DSA_HEAD_DIM], (((1,), (1,)), ((), ())),
                                  preferred_element_type=F32)
            bias = jnp.concatenate([bias_ref[h, bt] for bt in tiles], axis=0)
            logit = jnp.where(sel, s_t * scale + bias, -jnp.inf)
            m_new = jnp.maximum(ms[h], jnp.max(logit, axis=0, keepdims=True))
            m_safe = jnp.where(m_new == -jnp.inf, 0.0, m_new)
            alpha = jnp.exp2(ms[h] - m_safe)
            p = jnp.exp2(logit - m_safe)
            new_ls.append(alpha * ls[h] + jnp.sum(p, axis=0, keepdims=True))
            new_ms.append(m_new)
            pv = jnp.dot(vt_ref[0, l0:l0 + DSA_HEAD_DIM, pl.ds(r0, DSA_AC)], p.astype(BF16),
                         preferred_element_type=F32)
            acc_refs[h][...] = acc_refs[h][...] * alpha + pv
        return tuple(new_ms), tuple(new_ls)

    m0 = tuple(jnp.full((1, DSA_QB), -jnp.inf, F32) for _ in range(DSA_HEADS))
    l0s = tuple(jnp.zeros((1, DSA_QB), F32) for _ in range(DSA_HEADS))
    _, ls = _chunk_loop(n_kc, 4, attend, (m0, l0s))
    o_ref[0] = jnp.concatenate([(acc_refs[h][...] / ls[h]).T for h in range(DSA_HEADS)], axis=1)


def _rel_bucket(dist):
    max_exact = REL_BUCKETS // 2
    n = jnp.maximum(dist, 0)
    nf = jnp.maximum(n, 1).astype(F32)
    large = max_exact + (jnp.log(nf / max_exact) / math.log(REL_MAX_DIST / max_exact)
                         * (REL_BUCKETS - max_exact)).astype(jnp.int32)
    large = jnp.minimum(large, REL_BUCKETS - 1)
    return jnp.where(n < max_exact, n, large)


def _bias_tiles(rel_bias):
    bt = DSA_QB
    span = 2 * bt
    dist = jnp.arange(-(bt - 1), N_BIAS_TILES * bt + 1)
    by_dist = (rel_bias[_rel_bucket(dist)].astype(F32) * LOG2E).T
    rows = jnp.stack([by_dist[:, j * bt:j * bt + span] for j in range(N_BIAS_TILES)], axis=1)
    rep = jnp.tile(rows, (1, 1, bt))[..., bt - 1:bt - 1 + bt * (span - 1)]
    return rep.reshape(DSA_HEADS, N_BIAS_TILES, bt, span - 1)[..., :bt]


def _dsa(q, qit, wt, ki, k, vt, bias, topk):
    b, t, _ = q.shape
    nqw = IDX_HEADS * DSA_QB
    t_pad = ((t + DSA_KC - 1) // DSA_KC) * DSA_KC
    kern = functools.partial(_dsa_kernel, topk=topk)
    return pl.pallas_call(
        kern,
        grid=(b, t // DSA_QB),
        in_specs=[pl.BlockSpec((1, DSA_QB, DSA_WIDTH), lambda i, j: (i, j, 0)),
                  pl.BlockSpec((1, IDX_DIM, nqw), lambda i, j: (i, 0, j)),
                  pl.BlockSpec((1, IDX_HEADS, DSA_QB), lambda i, j: (i, 0, j)),
                  pl.BlockSpec((1, t, IDX_DIM), lambda i, j: (i, 0, 0)),
                  pl.BlockSpec((1, t, DSA_WIDTH), lambda i, j: (i, 0, 0)),
                  pl.BlockSpec((1, DSA_WIDTH, t), lambda i, j: (i, 0, 0)),
                  pl.BlockSpec(bias.shape, lambda i, j: (0, 0, 0, 0))],
        out_specs=pl.BlockSpec((1, DSA_QB, DSA_WIDTH), lambda i, j: (i, j, 0)),
        out_shape=jax.ShapeDtypeStruct((b, t, DSA_WIDTH), F32),
        scratch_shapes=[pltpu.VMEM((t_pad, DSA_QB), I32)]
        + [pltpu.VMEM((DSA_HEAD_DIM, DSA_QB), F32)] * DSA_HEADS,
        compiler_params=_params(("parallel", "arbitrary")),
    )(q, qit, wt, ki, k, vt, bias)


def _top16_rows(s, n_rows):
    tm = s.shape[1]
    rid = lax.broadcasted_iota(I32, (n_rows, tm), 0)
    vals, idxs = [], []
    for _ in range(PEER_TOPK):
        m = jnp.max(s, axis=0, keepdims=True)
        ix = jnp.min(jnp.where(s == m, rid, n_rows), axis=0, keepdims=True)
        vals.append(m)
        idxs.append(ix)
        s = jnp.where(rid == ix, -jnp.inf, s)
    return jnp.concatenate(vals, axis=0), jnp.concatenate(idxs, axis=0)


def _mix_kernel(x_ref, oa_ref, ob_ref, wo_ref, g2_ref, wq_ref, sk1_ref, sk2_ref,
                x1_ref, h2_ref, eidx_ref, gates_ref):
    tm = x_ref.shape[0]
    o = jnp.concatenate([oa_ref[...], ob_ref[...]], axis=1)
    x1 = x_ref[...] + _bdot(o, wo_ref[...])
    x1_ref[...] = x1
    h2 = _rms(x1, g2_ref[...])
    for j in range(SUBLANES):
        h2_ref[:, j, :] = h2[:, j * LANES:(j + 1) * LANES]
    query = _bdot(h2, wq_ref[...])
    half = PEER_KEY_DIM // 2
    groups = [(0, PEER_TOPK)] + [(a, PEER_TOPK // (a + 1)) for a in range(1, SUBLANES)]
    n_cand = PEER_TOPK + SUBLANES * SUBLANES
    pid = lax.broadcasted_iota(I32, (n_cand, tm), 0)
    row8 = lax.broadcasted_iota(I32, (SUBLANES, tm), 0)
    e_rows, g_rows = [], []
    for h in range(PEER_HEADS):
        q1 = query[:, h * PEER_KEY_DIM:h * PEER_KEY_DIM + half]
        q2 = query[:, h * PEER_KEY_DIM + half:(h + 1) * PEER_KEY_DIM]
        s1 = _bdot_nt(sk1_ref[h], q1)
        s2 = _bdot_nt(sk2_ref[h], q2)
        v1, i1 = _top16_rows(s1, PEER_KEYS)
        v2, i2 = _top16_rows(s2, PEER_KEYS)
        cands, cidxs = [], []
        for a, nb in groups:
            rows = PEER_TOPK if a == 0 else SUBLANES
            cv = v1[a:a + 1] + v2[:rows]
            ci = i1[a:a + 1] * PEER_KEYS + i2[:rows]
            if nb < rows:
                cv = jnp.where(row8 < nb, cv, -jnp.inf)
            cands.append(cv)
            cidxs.append(ci)
        cands.append(v1[SUBLANES:] + v2[0:1])
        cidxs.append(i1[SUBLANES:] * PEER_KEYS + i2[0:1])
        cand = jnp.concatenate(cands, axis=0)
        cidx = jnp.concatenate(cidxs, axis=0)
        tops, eids = [], []
        for _ in range(PEER_TOPK):
            m = jnp.max(cand, axis=0, keepdims=True)
            pos = jnp.min(jnp.where(cand == m, pid, n_cand), axis=0, keepdims=True)
            hit = pid == pos
            eids.append(jnp.max(jnp.where(hit, cidx, -1), axis=0, keepdims=True))
            tops.append(m)
            cand = jnp.where(hit, -jnp.inf, cand)
        top_s = jnp.concatenate(tops, axis=0)
        ex = jnp.exp(top_s - top_s[0:1])
        g_rows.append(ex / jnp.sum(ex, axis=0, keepdims=True))
        e_rows.append(jnp.concatenate(eids, axis=0))
    gates_ref[...] = jnp.concatenate(g_rows, axis=0).T
    eidx_ref[...] = (jnp.concatenate(e_rows, axis=0) * PEER_SLAB).T


def _mix(x2, o_a, o_b, w_out, ln2_g, w_query, sk1, sk2, tm):
    n = x2.shape[0]
    wo = w_out.astype(BF16)
    wq = w_query.astype(BF16)
    sk1 = sk1.astype(BF16)
    sk2 = sk2.astype(BF16)
    nsel = PEER_HEADS * PEER_TOPK
    row = lambda w: pl.BlockSpec((tm, w), lambda i: (i, 0))
    full = lambda a: pl.BlockSpec(a.shape, lambda i: (0,) * a.ndim)
    g2 = ln2_g.reshape(1, D_MODEL)
    return pl.pallas_call(
        _mix_kernel,
        grid=(n // tm,),
        in_specs=[row(D_MODEL), row(GDN_WIDTH), row(DSA_WIDTH), full(wo), full(g2), full(wq),
                  full(sk1), full(sk2)],
        out_specs=[row(D_MODEL), pl.BlockSpec((tm, SUBLANES, LANES), lambda i: (i, 0, 0)), row(nsel), row(nsel)],
        out_shape=[jax.ShapeDtypeStruct((n, D_MODEL), F32),
                   jax.ShapeDtypeStruct((n, SUBLANES, LANES), F32),
                   jax.ShapeDtypeStruct((n, nsel), I32),
                   jax.ShapeDtypeStruct((n, nsel), F32)],
        compiler_params=_params(("parallel",)),
    )(x2, o_a, o_b, wo, g2, wq, sk1, sk2)


PEER_TB = 32
PEER_SLAB = 4
PEER_HALF = D_MODEL // 2
PEER_NSEL = PEER_HEADS * PEER_TOPK
PEER_COLS = PEER_NSEL * PEER_SLAB


def _pack_table(w):
    e = w.shape[0]
    bits = lax.bitcast_convert_type(w.astype(jnp.bfloat16), jnp.uint16).astype(jnp.uint32)
    words = bits[:, :PEER_HALF] | (bits[:, PEER_HALF:] << 16)
    return lax.bitcast_convert_type(words, I32).reshape(e * PEER_SLAB, LANES)


def _unpack_words(words):
    lo = lax.bitcast_convert_type(words << 16, F32)
    hi = lax.bitcast_convert_type(words & jnp.int32(-65536), F32)
    return lo, hi


def _split_bf16(x):
    head = x.astype(jnp.bfloat16).astype(F32)
    return head, x - head


def _gather_slabs(idx_ref, row, tab_ref):
    slabs = []
    for k in range(PEER_NSEL):
        start = pl.multiple_of(idx_ref[row, k], PEER_SLAB)
        slabs.append(tab_ref[pl.ds(start, PEER_SLAB), :])
    return jnp.concatenate(slabs, axis=0)


def _load_table(tab_hbm, tab_ref, sem):
    @pl.when(pl.program_id(0) == 0)
    def _():
        cp = pltpu.make_async_copy(tab_hbm, tab_ref, sem)
        cp.start()
        cp.wait()


def _index_copy(idx_hbm, block, idx_ref, sem):
    return pltpu.make_async_copy(idx_hbm.at[pl.ds(block * PEER_TB, PEER_TB)], idx_ref, sem)


def _peer_blocks(idx_hbm, idx_a, idx_b, sems, process):
    i = pl.program_id(0)
    last = pl.num_programs(0) - 1

    @pl.when(i == 0)
    def _():
        _index_copy(idx_hbm, 0, idx_a, sems.at[0]).start()

    _index_copy(idx_hbm, 2 * i + 1, idx_b, sems.at[1]).start()
    _index_copy(idx_hbm, 2 * i, idx_a, sems.at[0]).wait()
    process(0, idx_a)

    @pl.when(i < last)
    def _():
        _index_copy(idx_hbm, 2 * i + 2, idx_a, sems.at[0]).start()

    _index_copy(idx_hbm, 2 * i + 1, idx_b, sems.at[1]).wait()
    process(1, idx_b)


def _slab_mask():
    j = lax.broadcasted_iota(I32, (SUBLANES, PEER_COLS), 0)
    col = lax.broadcasted_iota(I32, (SUBLANES, PEER_COLS), 1)
    return (col % PEER_SLAB) == (j % PEER_SLAB)


def _peer_u_kernel(x3_ref, gates_ref, rep_ref, gsum_ref, idx_hbm, tab_hbm, c_ref,
                   tab_ref, idx_a, idx_b, z_ref, sem, isems):
    _load_table(tab_hbm, tab_ref, sem)
    mask = _slab_mask()

    def process(half, idx_ref):
        for t in range(PEER_TB):
            tok = half * PEER_TB + t
            xh, xt = _split_bf16(x3_ref[tok])
            a_lo = jnp.concatenate([xh[:PEER_SLAB], xt[:PEER_SLAB]], axis=0)
            a_hi = jnp.concatenate([xh[PEER_SLAB:], xt[PEER_SLAB:]], axis=0)
            lo, hi = _unpack_words(_gather_slabs(idx_ref, t, tab_ref))
            r = _bdot_nt(a_lo, lo) + _bdot_nt(a_hi, hi)
            z_ref[tok:tok + 1, :] = jnp.sum(jnp.where(mask, r, 0.0), axis=0, keepdims=True)

    _peer_blocks(idx_hbm, idx_a, idx_b, isems, process)

    gh, gt = _split_bf16(gates_ref[...])
    rep = rep_ref[...]
    gates_rep = _bdot(gh, rep) + _bdot(gt, rep)
    zh, zt = _split_bf16(z_ref[...])
    gsum = gsum_ref[...]
    z = _bdot(zh, gsum) + _bdot(zt, gsum)
    c_ref[...] = 0.5 * z * (1.0 + lax.erf(z * (2.0 ** -0.5))) * gates_rep


def _peer_v_kernel(c_in_ref, x1_ref, fg_ref, idx_hbm, tab_hbm, o_ref,
                   tab_ref, idx_a, idx_b, sem, isems):
    _load_table(tab_hbm, tab_ref, sem)
    mask = _slab_mask()
    rows = lax.broadcasted_iota(I32, (SUBLANES, PEER_COLS), 0)

    def process(half, idx_ref):
        for t in range(PEER_TB):
            tok = half * PEER_TB + t
            ch, ct = _split_bf16(c_in_ref[tok:tok + 1, :])
            c8 = jnp.where(mask, jnp.where(rows < PEER_SLAB, ch, ct), 0.0)
            lo, hi = _unpack_words(_gather_slabs(idx_ref, t, tab_ref))
            y_lo = _bdot(c8, lo)
            y_hi = _bdot(c8, hi)
            y = jnp.concatenate([y_lo[:PEER_SLAB] + y_lo[PEER_SLAB:],
                                 y_hi[:PEER_SLAB] + y_hi[PEER_SLAB:]], axis=0)
            for j in range(SUBLANES):
                o_ref[tok:tok + 1, j * LANES:(j + 1) * LANES] = y[j:j + 1]

    _peer_blocks(idx_hbm, idx_a, idx_b, isems, process)
    o_ref[...] = _rms(x1_ref[...] + o_ref[...], fg_ref[...])


def _peer(slab_row, x1, h3, gates, final_g, peer_u, peer_v):
    n = x1.shape[0]
    step = 2 * PEER_TB
    u_tab = _pack_table(peer_u)
    v_tab = _pack_table(peer_v)
    fg = final_g.reshape(1, D_MODEL)
    group = jnp.arange(PEER_COLS) // PEER_SLAB
    rep = (group[None, :] == jnp.arange(PEER_NSEL)[:, None]).astype(BF16)
    gsum = (group[None, :] == group[:, None]).astype(BF16)
    tok = pl.BlockSpec((step, D_MODEL), lambda i: (i, 0))
    sel = pl.BlockSpec((step, PEER_NSEL), lambda i: (i, 0))
    wide = pl.BlockSpec((step, PEER_COLS), lambda i: (i, 0))
    full = lambda a: pl.BlockSpec(a.shape, lambda i: (0,) * a.ndim)
    hbm = pl.BlockSpec(memory_space=pl.ANY)
    idx_buf = pltpu.SMEM((PEER_TB, PEER_NSEL), I32)
    coef = pl.pallas_call(
        _peer_u_kernel,
        grid=(n // step,),
        in_specs=[pl.BlockSpec((step, SUBLANES, LANES), lambda i: (i, 0, 0)), sel, full(rep), full(gsum), hbm, hbm],
        out_specs=wide,
        out_shape=jax.ShapeDtypeStruct((n, PEER_COLS), F32),
        scratch_shapes=[pltpu.VMEM(u_tab.shape, I32), idx_buf, idx_buf,
                        pltpu.VMEM((step, PEER_COLS), F32),
                        pltpu.SemaphoreType.DMA(()), pltpu.SemaphoreType.DMA((2,))],
        compiler_params=_params(("arbitrary",)),
    )(h3, gates, rep, gsum, slab_row, u_tab)
    return pl.pallas_call(
        _peer_v_kernel,
        grid=(n // step,),
        in_specs=[wide, tok, full(fg), hbm, hbm],
        out_specs=tok,
        out_shape=jax.ShapeDtypeStruct((n, D_MODEL), F32),
        scratch_shapes=[pltpu.VMEM(v_tab.shape, I32), idx_buf, idx_buf,
                        pltpu.SemaphoreType.DMA(()), pltpu.SemaphoreType.DMA((2,))],
        compiler_params=_params(("arbitrary",)),
    )(coef, x1, fg, slab_row, v_tab)


def _permute_w_in(w_in):
    o = [0]
    for s in (GDN_WIDTH, GDN_WIDTH, GDN_WIDTH, GDN_WIDTH, GDN_HEADS, GDN_HEADS, Q_RANK, KV_RANK,
              IDX_DIM, IDX_HEADS):
        o.append(o[-1] + s)
    gq_gz = w_in[:, o[0]:o[4]]
    ga = w_in[:, o[4]:o[5]]
    gb = w_in[:, o[5]:o[6]]
    cq_ckv = w_in[:, o[6]:o[8]]
    kidx = w_in[:, o[8]:o[9]]
    widx = w_in[:, o[9]:o[10]]
    pad = jnp.zeros((w_in.shape[0], LANES - (IDX_DIM + IDX_HEADS + 2 * GDN_HEADS)), w_in.dtype)
    return jnp.concatenate([gq_gz, cq_ckv, kidx, widx, ga, gb, pad], axis=1).astype(BF16)


def kernel(x, ln1_g, w_in, conv_w, a_log, dt_bias, gdn_norm_g, q_norm_g, kv_norm_g, w_q_up,
           w_qidx_up, w_kv_up, idx_ln_g, idx_ln_b, w_out, ln2_g, peer_w_query, peer_sub_keys_1,
           peer_sub_keys_2, peer_u, peer_v, rel_bias, final_g):
    b, t, d = x.shape
    n = b * t
    assert w_in.shape[0] == 1, "single-layer block only"
    l = 0
    topk = min(IDX_TOPK_MAX, t // 4)
    bias = _bias_tiles(rel_bias)
    xc = x.reshape(n, d)
    gdn, dsa, small = _inproj(xc, ln1_g[l], _permute_w_in(w_in[l]), tm=256)
    gdn3 = gdn.reshape(b, t, -1)
    dsa3 = dsa.reshape(b, t, -1)
    small3 = small.reshape(b, t, LANES)
    o_a = _gdn(gdn3, small3, conv_w[l], a_log[l], dt_bias[l], gdn_norm_g[l], tb=256)
    q, qit, k, vt, ki, wt = _dsa_prep(dsa3, small3, q_norm_g[l], kv_norm_g[l], w_q_up[l],
                                      w_qidx_up[l], w_kv_up[l], idx_ln_g[l], idx_ln_b[l], tm=512)
    o_b = _dsa(q, qit, wt, ki, k, vt, bias, topk)
    x1, h3, slab_row, gates = _mix(xc, o_a.reshape(n, -1), o_b.reshape(n, -1), w_out[l], ln2_g[l],
                                   peer_w_query[l], peer_sub_keys_1[l], peer_sub_keys_2[l], tm=256)
    out = _peer(slab_row, x1, h3, gates, final_g, peer_u[l], peer_v[l])
    return out.reshape(b, t, d)
```

```python
import functools
import math

import jax
import jax.numpy as jnp
from jax import lax
from jax.experimental import pallas as pl
from jax.experimental.pallas import tpu as pltpu

F32 = jnp.float32
BF16 = jnp.bfloat16
I32 = jnp.int32

D_MODEL = 1024
GDN_HEADS = 4
GDN_HEAD_DIM = 128
GDN_WIDTH = GDN_HEADS * GDN_HEAD_DIM
CONV_WIDTH = 4
GDN_CHUNK = 64
DSA_HEADS = 4
DSA_HEAD_DIM = 128
DSA_WIDTH = DSA_HEADS * DSA_HEAD_DIM
Q_RANK = 256
KV_RANK = 256
IDX_HEADS = 16
IDX_DIM = 64
IDX_TOPK_MAX = 256
REL_BUCKETS = 32
REL_MAX_DIST = 1024
PEER_HEADS = 8
PEER_KEYS = 128
PEER_KEY_DIM = 256
PEER_TOPK = 16
EPS = 1e-6

LANES = 128
SUBLANES = 8
VMEM_LIMIT = 56 * 1024 * 1024

SM_KIDX = 0
SM_WIDX = 64
SM_GA = 80
SM_GB = 84
INT_MIN = -2 ** 31
LOG2E = math.log2(math.e)


def _bdot(a, b):
    return jnp.dot(a.astype(BF16), b.astype(BF16), preferred_element_type=F32)


def _bdot_nt(a, b):
    return lax.dot_general(a.astype(BF16), b.astype(BF16), (((1,), (1,)), ((), ())),
                           preferred_element_type=F32)


def _bdot_tn(a, b):
    return lax.dot_general(a.astype(BF16), b.astype(BF16), (((0,), (0,)), ((), ())),
                           preferred_element_type=F32)


def _hdot(a, b):
    ah = a.astype(jnp.bfloat16)
    bh = b.astype(jnp.bfloat16)
    al = (a - ah.astype(F32)).astype(jnp.bfloat16)
    bl = (b - bh.astype(F32)).astype(jnp.bfloat16)
    dot = functools.partial(jnp.dot, preferred_element_type=F32)
    return dot(ah, bh) + (dot(ah, bl) + dot(al, bh))


def _rms(x, g):
    return x * lax.rsqrt(jnp.mean(x * x, axis=-1, keepdims=True) + EPS) * g


def _silu(x):
    return x * (1.0 / (1.0 + jnp.exp(-x)))


def _chunk_loop(n, unroll, body, carry):
    def group(j, c):
        for u in range(unroll):
            c = body(unroll * j + u, c)
        return c

    carry = lax.fori_loop(0, n // unroll, group, carry)
    return lax.fori_loop((n // unroll) * unroll, n, body, carry)


def _params(sem):
    return pltpu.CompilerParams(dimension_semantics=sem, vmem_limit_bytes=VMEM_LIMIT)


def _inproj_kernel(x_ref, g_ref, w_ref, gdn_ref, dsa_ref, small_ref):
    h = _rms(x_ref[...], g_ref[...])
    p = _bdot(h, w_ref[...])
    n_gdn = gdn_ref.shape[-1]
    n_dsa = dsa_ref.shape[-1]
    gdn_ref[...] = p[:, :n_gdn]
    dsa_ref[...] = p[:, n_gdn:n_gdn + n_dsa]
    small_ref[...] = p[:, n_gdn + n_dsa:]


def _inproj(x2, ln1_g, w_perm, tm):
    n = x2.shape[0]
    n_gdn = 4 * GDN_WIDTH
    n_dsa = Q_RANK + KV_RANK
    return pl.pallas_call(
        _inproj_kernel,
        grid=(n // tm,),
        in_specs=[pl.BlockSpec((tm, D_MODEL), lambda i: (i, 0)),
                  pl.BlockSpec((1, D_MODEL), lambda i: (0, 0)),
                  pl.BlockSpec(w_perm.shape, lambda i: (0, 0))],
        out_specs=[pl.BlockSpec((tm, n_gdn), lambda i: (i, 0)),
                   pl.BlockSpec((tm, n_dsa), lambda i: (i, 0)),
                   pl.BlockSpec((tm, LANES), lambda i: (i, 0))],
        out_shape=[jax.ShapeDtypeStruct((n, n_gdn), F32),
                   jax.ShapeDtypeStruct((n, n_dsa), F32),
                   jax.ShapeDtypeStruct((n, LANES), F32)],
        compiler_params=_params(("parallel",)),
    )(x2, ln1_g.reshape(1, D_MODEL), w_perm)


def _gdn_kernel(gdn_ref, small_ref, convw_ref, alog_ref, dtb_ref, ng_ref, o_ref,
                tail_ref, state_ref):
    tb = gdn_ref.shape[1]
    c = GDN_CHUNK
    n_chunks = tb // c
    w3 = 3 * GDN_WIDTH

    @pl.when(pl.program_id(1) == 0)
    def _():
        tail_ref[...] = jnp.zeros_like(tail_ref)
        state_ref[...] = jnp.zeros_like(state_ref)

    blk = gdn_ref[0]
    xin = blk[:, :w3]
    z = blk[:, w3:]
    tail = tail_ref[...]
    cw = convw_ref[...]
    acc = xin * cw[CONV_WIDTH - 1:CONV_WIDTH]
    rows8 = lax.broadcasted_iota(I32, (SUBLANES, w3), 0)
    for k in range(1, CONV_WIDTH):
        xk = pltpu.roll(xin, k, axis=0)
        fix = pltpu.roll(tail, k, axis=0)
        top = jnp.where(rows8 < k, fix, xk[:SUBLANES])
        xk = jnp.concatenate([top, xk[SUBLANES:]], axis=0)
        acc = acc + xk * cw[CONV_WIDTH - 1 - k:CONV_WIDTH - k]
    tail_ref[...] = xin[tb - SUBLANES:]
    qkv = _silu(acc)

    sm = small_ref[0]
    sp = sm + dtb_ref[...]
    softplus = jnp.maximum(sp, 0.0) + jnp.log(1.0 + jnp.exp(-jnp.abs(sp)))
    gl = -jnp.exp(alog_ref[...]) * softplus
    beta = 1.0 / (1.0 + jnp.exp(-sm))
    rin = lax.broadcasted_iota(I32, (tb, LANES), 0) % c
    s = 1
    while s < c:
        gl = gl + jnp.where(rin >= s, pltpu.roll(gl, s, axis=0), 0.0)
        s *= 2
    g_t = gl.T
    eg = jnp.exp(gl)

    ri = lax.broadcasted_iota(I32, (c, c), 0)
    ci = lax.broadcasted_iota(I32, (c, c), 1)
    tril = ri >= ci
    strict = ri > ci
    eye = (ri == ci).astype(F32)
    ng = ng_ref[...]
    scale = GDN_HEAD_DIM ** -0.5

    probs = [(ch, h) for ch in range(n_chunks) for h in range(GDN_HEADS)]
    qs, ks, gcols, egcols, decays, kbs, vbs, ms, t_invs = ({} for _ in range(9))
    for ch, h in probs:
        r0, l0 = ch * c, h * GDN_HEAD_DIM
        qh = qkv[r0:r0 + c, l0:l0 + GDN_HEAD_DIM]
        kh = qkv[r0:r0 + c, GDN_WIDTH + l0:GDN_WIDTH + l0 + GDN_HEAD_DIM]
        vh = qkv[r0:r0 + c, 2 * GDN_WIDTH + l0:2 * GDN_WIDTH + l0 + GDN_HEAD_DIM]
        qs[ch, h] = qh * lax.rsqrt(jnp.sum(qh * qh, axis=-1, keepdims=True) + EPS) * scale
        kh = kh * lax.rsqrt(jnp.sum(kh * kh, axis=-1, keepdims=True) + EPS)
        ks[ch, h] = kh
        gcol = gl[r0:r0 + c, SM_GA + h:SM_GA + h + 1]
        grow = g_t[SM_GA + h:SM_GA + h + 1, r0:r0 + c]
        gcols[ch, h] = gcol
        egcols[ch, h] = eg[r0:r0 + c, SM_GA + h:SM_GA + h + 1]
        bcol = beta[r0:r0 + c, SM_GB + h:SM_GB + h + 1]
        decays[ch, h] = jnp.exp(jnp.where(tril, gcol - grow, -jnp.inf))
        kbs[ch, h] = kh * bcol
        vbs[ch, h] = vh * bcol
    for p in probs:
        a_mat = jnp.where(strict, _bdot_nt(kbs[p], ks[p]) * decays[p], 0.0)
        ms[p] = -a_mat
        t_invs[p] = eye + ms[p]
    step = 1
    while step < c // 2:
        for p in probs:
            ms[p] = _hdot(ms[p], ms[p])
        for p in probs:
            t_invs[p] = t_invs[p] + _hdot(t_invs[p], ms[p])
        step *= 2
    us = {p: _bdot(t_invs[p], vbs[p]) for p in probs}
    ws = {p: _bdot(t_invs[p], kbs[p] * egcols[p]) for p in probs}
    attns = {p: _bdot_nt(qs[p], ks[p]) * decays[p] for p in probs}

    out_rows = []
    for ch in range(n_chunks):
        r0 = ch * c
        out_heads = []
        for h in range(GDN_HEADS):
            p = (ch, h)
            l0 = h * GDN_HEAD_DIM
            st = state_ref[h]
            v_new = us[p] - _bdot(ws[p], st)
            o = _bdot(qs[p] * egcols[p], st) + _bdot(attns[p], v_new)
            glast = gcols[p][c - 1:c]
            state_ref[h] = st * jnp.exp(glast) + _bdot_tn(ks[p] * jnp.exp(glast - gcols[p]), v_new)
            zh = z[r0:r0 + c, l0:l0 + GDN_HEAD_DIM]
            out_heads.append(_rms(o, ng) * _silu(zh))
        out_rows.append(jnp.concatenate(out_heads, axis=1))
    o_ref[0] = jnp.concatenate(out_rows, axis=0)


def _gdn(gdn3, small3, conv_w, a_log, dt_bias, norm_g, tb):
    b, t, _ = gdn3.shape
    alog_pad = jnp.zeros((1, LANES), F32).at[0, SM_GA:SM_GA + GDN_HEADS].set(a_log)
    dtb_pad = jnp.zeros((1, LANES), F32).at[0, SM_GA:SM_GA + GDN_HEADS].set(dt_bias)
    return pl.pallas_call(
        _gdn_kernel,
        grid=(b, t // tb),
        in_specs=[pl.BlockSpec((1, tb, 4 * GDN_WIDTH), lambda i, j: (i, j, 0)),
                  pl.BlockSpec((1, tb, LANES), lambda i, j: (i, j, 0)),
                  pl.BlockSpec((CONV_WIDTH, 3 * GDN_WIDTH), lambda i, j: (0, 0)),
                  pl.BlockSpec((1, LANES), lambda i, j: (0, 0)),
                  pl.BlockSpec((1, LANES), lambda i, j: (0, 0)),
                  pl.BlockSpec((1, GDN_HEAD_DIM), lambda i, j: (0, 0))],
        out_specs=pl.BlockSpec((1, tb, GDN_WIDTH), lambda i, j: (i, j, 0)),
        out_shape=jax.ShapeDtypeStruct((b, t, GDN_WIDTH), F32),
        scratch_shapes=[pltpu.VMEM((SUBLANES, 3 * GDN_WIDTH), F32),
                        pltpu.VMEM((GDN_HEADS, GDN_HEAD_DIM, GDN_HEAD_DIM), F32)],
        compiler_params=_params(("parallel", "arbitrary")),
    )(gdn3, small3, conv_w, alog_pad, dtb_pad, norm_g.reshape(1, GDN_HEAD_DIM))


def _dsa_prep_kernel(dsa_ref, small_ref, qg_ref, kvg_ref, wq_ref, wqi_t_ref, wk_ref, wv_t_ref,
                     lng_ref, lnb_ref, q_ref, qit_ref, k_ref, vt_ref, ki_ref, wt_ref):
    tm = dsa_ref.shape[1]
    blk = dsa_ref[0]
    cq = _rms(blk[:, :Q_RANK], qg_ref[...])
    ckv = _rms(blk[:, Q_RANK:], kvg_ref[...])
    q_ref[0] = _bdot(cq, wq_ref[...]).astype(BF16)
    k_ref[0] = _bdot(ckv, wk_ref[...]).astype(BF16)
    vt_ref[0] = _bdot_nt(wv_t_ref[...], ckv).astype(BF16)
    qit = _bdot_nt(wqi_t_ref[...], cq)
    nqb = tm // LANES
    cols = []
    for j in range(nqb):
        for h in range(IDX_HEADS):
            cols.append(qit[h * IDX_DIM:(h + 1) * IDX_DIM, j * LANES:(j + 1) * LANES])
    qit_ref[0] = jnp.concatenate(cols, axis=1).astype(BF16)
    sm = small_ref[0]
    kx = sm[:, SM_KIDX:SM_KIDX + IDX_DIM]
    mu = jnp.mean(kx, axis=-1, keepdims=True)
    xc = kx - mu
    kn = xc * lax.rsqrt(jnp.mean(xc * xc, axis=-1, keepdims=True) + EPS)
    ki_ref[0] = (kn * lng_ref[...] + lnb_ref[...]).astype(BF16)
    sm_t = sm.T
    wt_ref[0] = sm_t[SM_WIDX:SM_WIDX + IDX_HEADS] * (IDX_HEADS ** -0.5 * IDX_DIM ** -0.5)


def _dsa_prep(dsa3, small3, q_norm_g, kv_norm_g, w_q_up, w_qidx_up, w_kv_up, idx_ln_g, idx_ln_b, tm):
    b, t, _ = dsa3.shape
    wq = w_q_up.astype(BF16)
    wqi_t = w_qidx_up.T.astype(BF16)
    wkv = w_kv_up.reshape(KV_RANK, DSA_HEADS, 2, DSA_HEAD_DIM)
    wk = wkv[:, :, 0].reshape(KV_RANK, DSA_WIDTH).astype(BF16)
    wv_t = wkv[:, :, 1].reshape(KV_RANK, DSA_WIDTH).T.astype(BF16)
    full = lambda a: pl.BlockSpec(a.shape, lambda i, j: (0,) * a.ndim)
    qg = q_norm_g.reshape(1, Q_RANK)
    kvg = kv_norm_g.reshape(1, KV_RANK)
    lng = idx_ln_g.reshape(1, IDX_DIM)
    lnb = idx_ln_b.reshape(1, IDX_DIM)
    nqw = IDX_HEADS * LANES
    return pl.pallas_call(
        _dsa_prep_kernel,
        grid=(b, t // tm),
        in_specs=[pl.BlockSpec((1, tm, Q_RANK + KV_RANK), lambda i, j: (i, j, 0)),
                  pl.BlockSpec((1, tm, LANES), lambda i, j: (i, j, 0)),
                  full(qg), full(kvg), full(wq), full(wqi_t), full(wk), full(wv_t),
                  full(lng), full(lnb)],
        out_specs=[pl.BlockSpec((1, tm, DSA_WIDTH), lambda i, j: (i, j, 0)),
                   pl.BlockSpec((1, IDX_DIM, (tm // LANES) * nqw), lambda i, j: (i, 0, j)),
                   pl.BlockSpec((1, tm, DSA_WIDTH), lambda i, j: (i, j, 0)),
                   pl.BlockSpec((1, DSA_WIDTH, tm), lambda i, j: (i, 0, j)),
                   pl.BlockSpec((1, tm, IDX_DIM), lambda i, j: (i, j, 0)),
                   pl.BlockSpec((1, IDX_HEADS, tm), lambda i, j: (i, 0, j))],
        out_shape=[jax.ShapeDtypeStruct((b, t, DSA_WIDTH), BF16),
                   jax.ShapeDtypeStruct((b, IDX_DIM, (t // LANES) * nqw), BF16),
                   jax.ShapeDtypeStruct((b, t, DSA_WIDTH), BF16),
                   jax.ShapeDtypeStruct((b, DSA_WIDTH, t), BF16),
                   jax.ShapeDtypeStruct((b, t, IDX_DIM), BF16),
                   jax.ShapeDtypeStruct((b, IDX_HEADS, t), F32)],
        compiler_params=_params(("parallel", "parallel")),
    )(dsa3, small3, qg, kvg, wq, wqi_t, wk, wv_t, lng, lnb)


DSA_QB = 128
DSA_KC = 512
DSA_AC = DSA_KC
_REL_EXACT = REL_BUCKETS // 2
REL_SATURATION = math.ceil(_REL_EXACT * (REL_MAX_DIST / _REL_EXACT)
                           ** ((REL_BUCKETS - 1 - _REL_EXACT) / (REL_BUCKETS - _REL_EXACT)))
N_BIAS_TILES = -(-(REL_SATURATION + DSA_QB - 1) // DSA_QB) + 1


def _dsa_kernel(q_ref, qit_ref, wt_ref, ki_ref, k_ref, vt_ref, bias_ref, o_ref,
                keys_ref, *acc_refs, topk):
    i = pl.program_id(1)
    t0 = i * DSA_QB
    n_kc = (t0 + DSA_QB + DSA_KC - 1) // DSA_KC
    tq = t0 + lax.broadcasted_iota(I32, (1, DSA_QB), 1)
    qit = qit_ref[0]
    wt = wt_ref[0]

    def score_chunk(kc, carry):
        r0 = pl.multiple_of(kc * DSA_KC, DSA_KC)
        big = jnp.dot(ki_ref[0, pl.ds(r0, DSA_KC), :], qit, preferred_element_type=F32)
        sc = jnp.zeros((DSA_KC, DSA_QB), F32)
        for h in range(IDX_HEADS):
            sc = sc + jnp.maximum(big[:, h * DSA_QB:(h + 1) * DSA_QB], 0.0) * wt[h:h + 1]
        bits = pltpu.bitcast(sc, I32)
        key = bits ^ ((bits >> 31) & 0x7FFFFFFF)
        spos = r0 + lax.broadcasted_iota(I32, (DSA_KC, 1), 0)
        keys_ref[pl.ds(r0, DSA_KC), :] = jnp.where(spos <= tq, key, INT_MIN)
        return carry

    _chunk_loop(n_kc, 2, score_chunk, 0)

    def count_ge(cand):
        def body(kc, cnt):
            r0 = pl.multiple_of(kc * DSA_KC, DSA_KC)
            ge = (keys_ref[pl.ds(r0, DSA_KC), :] >= cand).astype(I32)
            return cnt + jnp.sum(ge.reshape(DSA_KC // SUBLANES, SUBLANES, DSA_QB), axis=0)
        cnt8 = lax.fori_loop(0, n_kc, body, jnp.zeros((SUBLANES, DSA_QB), I32))
        return jnp.sum(cnt8, axis=0, keepdims=True)

    zero = jnp.zeros((1, DSA_QB), I32)
    ans = jnp.where(count_ge(zero) >= topk, zero, jnp.full((1, DSA_QB), INT_MIN, I32))

    def bit_pass(b, ans):
        cand = ans | (jnp.int32(1) << (30 - b))
        return jnp.where(count_ge(cand) >= topk, cand, ans)

    ans = lax.fori_loop(0, 31, bit_pass, ans)
    thr = jnp.maximum(ans, INT_MIN + 1)

    n_ge = count_ge(thr)

    @pl.when(jnp.max(n_ge) > topk)
    def _():
        need = topk - count_ge(thr + 1)

        def tied_before(pos):
            def body(kc, cnt):
                r0 = pl.multiple_of(kc * DSA_KC, DSA_KC)
                spos = r0 + lax.broadcasted_iota(I32, (DSA_KC, 1), 0)
                hit = ((keys_ref[pl.ds(r0, DSA_KC), :] == thr) & (spos < pos)).astype(I32)
                return cnt + jnp.sum(hit.reshape(DSA_KC // SUBLANES, SUBLANES, DSA_QB), axis=0)
            cnt8 = lax.fori_loop(0, n_kc, body, jnp.zeros((SUBLANES, DSA_QB), I32))
            return jnp.sum(cnt8, axis=0, keepdims=True)

        def pos_bit(b, lo):
            cand = lo + (jnp.int32(1) << (pos_bits - 1 - b))
            return jnp.where(tied_before(cand) < need, cand, lo)

        pos_bits = keys_ref.shape[0].bit_length()
        last = lax.fori_loop(0, pos_bits, pos_bit, jnp.zeros((1, DSA_QB), I32))

        def retire(kc, carry):
            r0 = pl.multiple_of(kc * DSA_KC, DSA_KC)
            spos = r0 + lax.broadcasted_iota(I32, (DSA_KC, 1), 0)
            key = keys_ref[pl.ds(r0, DSA_KC), :]
            keys_ref[pl.ds(r0, DSA_KC), :] = jnp.where((key == thr) & (spos > last), INT_MIN, key)
            return carry

        lax.fori_loop(0, n_kc, retire, 0)

    for h in range(DSA_HEADS):
        acc_refs[h][...] = jnp.zeros((DSA_HEAD_DIM, DSA_QB), F32)
    qb = q_ref[0]
    scale = DSA_HEAD_DIM ** -0.5 * LOG2E
    sub = DSA_AC // DSA_QB

    def attend(kc, carry):
        ms, ls = carry
        r0 = pl.multiple_of(kc * DSA_AC, DSA_AC)
        sel = keys_ref[pl.ds(r0, DSA_AC), :] >= thr
        tiles = [jnp.clip(i - (kc * sub + a), 0, N_BIAS_TILES - 1) for a in range(sub)]
        new_ms, new_ls = [], []
        for h in range(DSA_HEADS):
            l0 = h * DSA_HEAD_DIM
            s_t = lax.dot_general(k_ref[0, pl.ds(r0, DSA_AC), l0:l0 + DSA_HEAD_DIM],
                                  qb[:, l0:l0 + DSA_HEAD_DIM], (((1,), (1,)), ((), ())),
                                  preferred_element_type=F32)
            bias = jnp.concatenate([bias_ref[h, bt] for bt in tiles], axis=0)
            logit = jnp.where(sel, s_t * scale + bias, -jnp.inf)
            m_new = jnp.maximum(ms[h], jnp.max(logit, axis=0, keepdims=True))
            m_safe = jnp.where(m_new == -jnp.inf, 0.0, m_new)
            alpha = jnp.exp2(ms[h] - m_safe)
            p = jnp.exp2(logit - m_safe)
            new_ls.append(alpha * ls[h] + jnp.sum(p, axis=0, keepdims=True))
            new_ms.append(m_new)
            pv = jnp.dot(vt_ref[0, l0:l0 + DSA_HEAD_DIM, pl.ds(r0, DSA_AC)], p.astype(BF16),
                         preferred_element_type=F32)
            acc_refs[h][...] = acc_refs[h][...] * alpha + pv
        return tuple(new_ms), tuple(new_ls)

    m0 = tuple(jnp.full((1, DSA_QB), -jnp.inf, F32) for _ in range(DSA_HEADS))
    l0s = tuple(jnp.zeros((1, DSA_QB), F32) for _ in range(DSA_HEADS))
    _, ls = _chunk_loop(n_kc, 4, attend, (m0, l0s))
    o_ref[0] = jnp.concatenate([(acc_refs[h][...] / ls[h]).T for h in range(DSA_HEADS)], axis=1)


def _rel_bucket(dist):
    max_exact = REL_BUCKETS // 2
    n = jnp.maximum(dist, 0)
    nf = jnp.maximum(n, 1).astype(F32)
    large = max_exact + (jnp.log(nf / max_exact) / math.log(REL_MAX_DIST / max_exact)
                         * (REL_BUCKETS - max_exact)).astype(jnp.int32)
    large = jnp.minimum(large, REL_BUCKETS - 1)
    return jnp.where(n < max_exact, n, large)


def _bias_tiles(rel_bias):
    bt = DSA_QB
    span = 2 * bt
    dist = jnp.arange(-(bt - 1), N_BIAS_TILES * bt + 1)
    by_dist = (rel_bias[_rel_bucket(dist)].astype(F32) * LOG2E).T
    rows = jnp.stack([by_dist[:, j * bt:j * bt + span] for j in range(N_BIAS_TILES)], axis=1)
    rep = jnp.tile(rows, (1, 1, bt))[..., bt - 1:bt - 1 + bt * (span - 1)]
    return rep.reshape(DSA_HEADS, N_BIAS_TILES, bt, span - 1)[..., :bt]


def _dsa(q, qit, wt, ki, k, vt, bias, topk):
    b, t, _ = q.shape
    nqw = IDX_HEADS * DSA_QB
    t_pad = ((t + DSA_KC - 1) // DSA_KC) * DSA_KC
    kern = functools.partial(_dsa_kernel, topk=topk)
    return pl.pallas_call(
        kern,
        grid=(b, t // DSA_QB),
        in_specs=[pl.BlockSpec((1, DSA_QB, DSA_WIDTH), lambda i, j: (i, j, 0)),
                  pl.BlockSpec((1, IDX_DIM, nqw), lambda i, j: (i, 0, j)),
                  pl.BlockSpec((1, IDX_HEADS, DSA_QB), lambda i, j: (i, 0, j)),
                  pl.BlockSpec((1, t, IDX_DIM), lambda i, j: (i, 0, 0)),
                  pl.BlockSpec((1, t, DSA_WIDTH), lambda i, j: (i, 0, 0)),
                  pl.BlockSpec((1, DSA_WIDTH, t), lambda i, j: (i, 0, 0)),
                  pl.BlockSpec(bias.shape, lambda i, j: (0, 0, 0, 0))],
        out_specs=pl.BlockSpec((1, DSA_QB, DSA_WIDTH), lambda i, j: (i, j, 0)),
        out_shape=jax.ShapeDtypeStruct((b, t, DSA_WIDTH), F32),
        scratch_shapes=[pltpu.VMEM((t_pad, DSA_QB), I32)]
        + [pltpu.VMEM((DSA_HEAD_DIM, DSA_QB), F32)] * DSA_HEADS,
        compiler_params=_params(("parallel", "arbitrary")),
    )(q, qit, wt, ki, k, vt, bias)


def _top16_rows(s, n_rows):
    tm = s.shape[1]
    rid = lax.broadcasted_iota(I32, (n_rows, tm), 0)
    vals, idxs = [], []
    for _ in range(PEER_TOPK):
        m = jnp.max(s, axis=0, keepdims=True)
        ix = jnp.min(jnp.where(s == m, rid, n_rows), axis=0, keepdims=True)
        vals.append(m)
        idxs.append(ix)
        s = jnp.where(rid == ix, -jnp.inf, s)
    return jnp.concatenate(vals, axis=0), jnp.concatenate(idxs, axis=0)


def _mix_kernel(x_ref, oa_ref, ob_ref, wo_ref, g2_ref, wq_ref, sk1_ref, sk2_ref,
                x1_ref, h2_ref, eidx_ref, gates_ref):
    tm = x_ref.shape[0]
    o = jnp.concatenate([oa_ref[...], ob_ref[...]], axis=1)
    x1 = x_ref[...] + _bdot(o, wo_ref[...])
    x1_ref[...] = x1
    h2 = _rms(x1, g2_ref[...])
    for j in range(SUBLANES):
        h2_ref[:, j, :] = h2[:, j * LANES:(j + 1) * LANES]
    query = _bdot(h2, wq_ref[...])
    half = PEER_KEY_DIM // 2
    groups = [(0, PEER_TOPK)] + [(a, PEER_TOPK // (a + 1)) for a in range(1, SUBLANES)]
    n_cand = PEER_TOPK + SUBLANES * SUBLANES
    pid = lax.broadcasted_iota(I32, (n_cand, tm), 0)
    row8 = lax.broadcasted_iota(I32, (SUBLANES, tm), 0)
    e_rows, g_rows = [], []
    for h in range(PEER_HEADS):
        q1 = query[:, h * PEER_KEY_DIM:h * PEER_KEY_DIM + half]
        q2 = query[:, h * PEER_KEY_DIM + half:(h + 1) * PEER_KEY_DIM]
        s1 = _bdot_nt(sk1_ref[h], q1)
        s2 = _bdot_nt(sk2_ref[h], q2)
        v1, i1 = _top16_rows(s1, PEER_KEYS)
        v2, i2 = _top16_rows(s2, PEER_KEYS)
        cands, cidxs = [], []
        for a, nb in groups:
            rows = PEER_TOPK if a == 0 else SUBLANES
            cv = v1[a:a + 1] + v2[:rows]
            ci = i1[a:a + 1] * PEER_KEYS + i2[:rows]
            if nb < rows:
                cv = jnp.where(row8 < nb, cv, -jnp.inf)
            cands.append(cv)
            cidxs.append(ci)
        cands.append(v1[SUBLANES:] + v2[0:1])
        cidxs.append(i1[SUBLANES:] * PEER_KEYS + i2[0:1])
        cand = jnp.concatenate(cands, axis=0)
        cidx = jnp.concatenate(cidxs, axis=0)
        tops, eids = [], []
        for _ in range(PEER_TOPK):
            m = jnp.max(cand, axis=0, keepdims=True)
            pos = jnp.min(jnp.where(cand == m, pid, n_cand), axis=0, keepdims=True)
            hit = pid == pos
            eids.append(jnp.max(jnp.where(hit, cidx, -1), axis=0, keepdims=True))
            tops.append(m)
            cand = jnp.where(hit, -jnp.inf, cand)
        top_s = jnp.concatenate(tops, axis=0)
        ex = jnp.exp(top_s - top_s[0:1])
        g_rows.append(ex / jnp.sum(ex, axis=0, keepdims=True))
        e_rows.append(jnp.concatenate(eids, axis=0))
    gates_ref[...] = jnp.concatenate(g_rows, axis=0).T
    eidx_ref[...] = (jnp.concatenate(e_rows, axis=0) * PEER_SLAB).T


def _mix(x2, o_a, o_b, w_out, ln2_g, w_query, sk1, sk2, tm):
    n = x2.shape[0]
    wo = w_out.astype(BF16)
    wq = w_query.astype(BF16)
    sk1 = sk1.astype(BF16)
    sk2 = sk2.astype(BF16)
    nsel = PEER_HEADS * PEER_TOPK
    row = lambda w: pl.BlockSpec((tm, w), lambda i: (i, 0))
    full = lambda a: pl.BlockSpec(a.shape, lambda i: (0,) * a.ndim)
    g2 = ln2_g.reshape(1, D_MODEL)
    return pl.pallas_call(
        _mix_kernel,
        grid=(n // tm,),
        in_specs=[row(D_MODEL), row(GDN_WIDTH), row(DSA_WIDTH), full(wo), full(g2), full(wq),
                  full(sk1), full(sk2)],
        out_specs=[row(D_MODEL), pl.BlockSpec((tm, SUBLANES, LANES), lambda i: (i, 0, 0)), row(nsel), row(nsel)],
        out_shape=[jax.ShapeDtypeStruct((n, D_MODEL), F32),
                   jax.ShapeDtypeStruct((n, SUBLANES, LANES), F32),
                   jax.ShapeDtypeStruct((n, nsel), I32),
                   jax.ShapeDtypeStruct((n, nsel), F32)],
        compiler_params=_params(("parallel",)),
    )(x2, o_a, o_b, wo, g2, wq, sk1, sk2)


PEER_TB = 64
PEER_SLAB = 4
PEER_HALF = D_MODEL // 2
PEER_NSEL = PEER_HEADS * PEER_TOPK
PEER_COLS = PEER_NSEL * PEER_SLAB


def _pack_table(w):
    e = w.shape[0]
    w4 = w.reshape(e, 2, PEER_SLAB, LANES)
    bits = lax.bitcast_convert_type(w4.astype(jnp.bfloat16), jnp.uint16).astype(jnp.uint32)
    words = bits[:, 0] | (bits[:, 1] << 16)
    return lax.bitcast_convert_type(words, I32).reshape(e * PEER_SLAB, LANES)


def _unpack_words(words):
    lo = lax.bitcast_convert_type(words << 16, F32)
    hi = lax.bitcast_convert_type(words & jnp.int32(-65536), F32)
    return lo, hi


def _split_bf16(x):
    head = x.astype(jnp.bfloat16).astype(F32)
    return head, x - head


def _gather_slabs(idx_ref, row, tab_ref):
    slabs = []
    for k in range(PEER_NSEL):
        start = pl.multiple_of(idx_ref[row, k], PEER_SLAB)
        slabs.append(tab_ref[pl.ds(start, PEER_SLAB), :])
    return jnp.concatenate(slabs, axis=0)


def _load_table(tab_hbm, tab_ref, sem):
    @pl.when(pl.program_id(0) == 0)
    def _():
        cp = pltpu.make_async_copy(tab_hbm, tab_ref, sem)
        cp.start()
        cp.wait()


def _index_copy(idx_hbm, block, idx_ref, sem):
    return pltpu.make_async_copy(idx_hbm.at[pl.ds(block * PEER_TB, PEER_TB)], idx_ref, sem)


def _peer_blocks(idx_hbm, idx_a, idx_b, sems, process):
    i = pl.program_id(0)
    last = pl.num_programs(0) - 1

    @pl.when(i == 0)
    def _():
        _index_copy(idx_hbm, 0, idx_a, sems.at[0]).start()

    _index_copy(idx_hbm, 2 * i + 1, idx_b, sems.at[1]).start()
    _index_copy(idx_hbm, 2 * i, idx_a, sems.at[0]).wait()
    process(0, idx_a)

    @pl.when(i < last)
    def _():
        _index_copy(idx_hbm, 2 * i + 2, idx_a, sems.at[0]).start()

    _index_copy(idx_hbm, 2 * i + 1, idx_b, sems.at[1]).wait()
    process(1, idx_b)


def _slab_mask():
    j = lax.broadcasted_iota(I32, (SUBLANES, PEER_COLS), 0)
    col = lax.broadcasted_iota(I32, (SUBLANES, PEER_COLS), 1)
    return (col % PEER_SLAB) == (j % PEER_SLAB)


def _peer_u_kernel(x3_ref, gates_ref, rep_ref, gsum_ref, idx_hbm, tab_hbm, c_ref,
                   tab_ref, idx_a, idx_b, z_ref, sem, isems):
    _load_table(tab_hbm, tab_ref, sem)
    mask = _slab_mask()

    def process(half, idx_ref):
        for t in range(PEER_TB):
            tok = half * PEER_TB + t
            xh, xt = _split_bf16(x3_ref[tok])
            a_lo = jnp.concatenate([xh[:PEER_SLAB], xt[:PEER_SLAB]], axis=0)
            a_hi = jnp.concatenate([xh[PEER_SLAB:], xt[PEER_SLAB:]], axis=0)
            lo, hi = _unpack_words(_gather_slabs(idx_ref, t, tab_ref))
            r = _bdot_nt(a_lo, lo) + _bdot_nt(a_hi, hi)
            z_ref[tok:tok + 1, :] = jnp.sum(jnp.where(mask, r, 0.0), axis=0, keepdims=True)

    _peer_blocks(idx_hbm, idx_a, idx_b, isems, process)

    gh, gt = _split_bf16(gates_ref[...])
    rep = rep_ref[...]
    gates_rep = _bdot(gh, rep) + _bdot(gt, rep)
    zh, zt = _split_bf16(z_ref[...])
    gsum = gsum_ref[...]
    z = _bdot(zh, gsum) + _bdot(zt, gsum)
    c_ref[...] = 0.5 * z * (1.0 + lax.erf(z * (2.0 ** -0.5))) * gates_rep


def _peer_v_kernel(c_in_ref, x1_ref, fg_ref, idx_hbm, tab_hbm, o_ref,
                   tab_ref, idx_a, idx_b, sem, isems):
    _load_table(tab_hbm, tab_ref, sem)
    mask = _slab_mask()
    rows = lax.broadcasted_iota(I32, (SUBLANES, PEER_COLS), 0)

    def process(half, idx_ref):
        for t in range(PEER_TB):
            tok = half * PEER_TB + t
            ch, ct = _split_bf16(c_in_ref[tok:tok + 1, :])
            c8 = jnp.where(mask, jnp.where(rows < PEER_SLAB, ch, ct), 0.0)
            lo, hi = _unpack_words(_gather_slabs(idx_ref, t, tab_ref))
            y_lo = _bdot(c8, lo)
            y_hi = _bdot(c8, hi)
            y = jnp.concatenate([y_lo[:PEER_SLAB] + y_lo[PEER_SLAB:],
                                 y_hi[:PEER_SLAB] + y_hi[PEER_SLAB:]], axis=0)
            for j in range(SUBLANES):
                o_ref[tok:tok + 1, j * LANES:(j + 1) * LANES] = y[j:j + 1]

    _peer_blocks(idx_hbm, idx_a, idx_b, isems, process)
    o_ref[...] = _rms(x1_ref[...] + o_ref[...], fg_ref[...])


def _peer(slab_row, x1, h3, gates, final_g, peer_u, peer_v):
    n = x1.shape[0]
    step = 2 * PEER_TB
    u_tab = _pack_table(peer_u)
    v_tab = _pack_table(peer_v)
    fg = final_g.reshape(1, D_MODEL)
    group = jnp.arange(PEER_COLS) // PEER_SLAB
    rep = (group[None, :] == jnp.arange(PEER_NSEL)[:, None]).astype(BF16)
    gsum = (group[None, :] == group[:, None]).astype(BF16)
    tok = pl.BlockSpec((step, D_MODEL), lambda i: (i, 0))
    sel = pl.BlockSpec((step, PEER_NSEL), lambda i: (i, 0))
    wide = pl.BlockSpec((step, PEER_COLS), lambda i: (i, 0))
    full = lambda a: pl.BlockSpec(a.shape, lambda i: (0,) * a.ndim)
    hbm = pl.BlockSpec(memory_space=pl.ANY)
    idx_buf = pltpu.SMEM((PEER_TB, PEER_NSEL), I32)
    coef = pl.pallas_call(
        _peer_u_kernel,
        grid=(n // step,),
        in_specs=[pl.BlockSpec((step, SUBLANES, LANES), lambda i: (i, 0, 0)), sel, full(rep), full(gsum), hbm, hbm],
        out_specs=wide,
        out_shape=jax.ShapeDtypeStruct((n, PEER_COLS), F32),
        scratch_shapes=[pltpu.VMEM(u_tab.shape, I32), idx_buf, idx_buf,
                        pltpu.VMEM((step, PEER_COLS), F32),
                        pltpu.SemaphoreType.DMA(()), pltpu.SemaphoreType.DMA((2,))],
        compiler_params=_params(("arbitrary",)),
    )(h3, gates, rep, gsum, slab_row, u_tab)
    return pl.pallas_call(
        _peer_v_kernel,
        grid=(n // step,),
        in_specs=[wide, tok, full(fg), hbm, hbm],
        out_specs=tok,
        out_shape=jax.ShapeDtypeStruct((n, D_MODEL), F32),
        scratch_shapes=[pltpu.VMEM(v_tab.shape, I32), idx_buf, idx_buf,
                        pltpu.SemaphoreType.DMA(()), pltpu.SemaphoreType.DMA((2,))],
        compiler_params=_params(("arbitrary",)),
    )(coef, x1, fg, slab_row, v_tab)


def _permute_w_in(w_in):
    o = [0]
    for s in (GDN_WIDTH, GDN_WIDTH, GDN_WIDTH, GDN_WIDTH, GDN_HEADS, GDN_HEADS, Q_RANK, KV_RANK,
              IDX_DIM, IDX_HEADS):
        o.append(o[-1] + s)
    gq_gz = w_in[:, o[0]:o[4]]
    ga = w_in[:, o[4]:o[5]]
    gb = w_in[:, o[5]:o[6]]
    cq_ckv = w_in[:, o[6]:o[8]]
    kidx = w_in[:, o[8]:o[9]]
    widx = w_in[:, o[9]:o[10]]
    pad = jnp.zeros((w_in.shape[0], LANES - (IDX_DIM + IDX_HEADS + 2 * GDN_HEADS)), w_in.dtype)
    return jnp.concatenate([gq_gz, cq_ckv, kidx, widx, ga, gb, pad], axis=1).astype(BF16)


def kernel(x, ln1_g, w_in, conv_w, a_log, dt_bias, gdn_norm_g, q_norm_g, kv_norm_g, w_q_up,
           w_qidx_up, w_kv_up, idx_ln_g, idx_ln_b, w_out, ln2_g, peer_w_query, peer_sub_keys_1,
           peer_sub_keys_2, peer_u, peer_v, rel_bias, final_g):
    b, t, d = x.shape
    n = b * t
    assert w_in.shape[0] == 1, "single-layer block only"
    l = 0
    topk = min(IDX_TOPK_MAX, t // 4)
    bias = _bias_tiles(rel_bias)
    xc = x.reshape(n, d)
    gdn, dsa, small = _inproj(xc, ln1_g[l], _permute_w_in(w_in[l]), tm=256)
    gdn3 = gdn.reshape(b, t, -1)
    dsa3 = dsa.reshape(b, t, -1)
    small3 = small.reshape(b, t, LANES)
    o_a = _gdn(gdn3, small3, conv_w[l], a_log[l], dt_bias[l], gdn_norm_g[l], tb=256)
    q, qit, k, vt, ki, wt = _dsa_prep(dsa3, small3, q_norm_g[l], kv_norm_g[l], w_q_up[l],
                                      w_qidx_up[l], w_kv_up[l], idx_ln_g[l], idx_ln_b[l], tm=512)
    o_b = _dsa(q, qit, wt, ki, k, vt, bias, topk)
    x1, h3, slab_row, gates = _mix(xc, o_a.reshape(n, -1), o_b.reshape(n, -1), w_out[l], ln2_g[l],
                                   peer_w_query[l], peer_sub_keys_1[l], peer_sub_keys_2[l], tm=256)
    out = _peer(slab_row, x1, h3, gates, final_g, peer_u[l], peer_v[l])
    return out.reshape(b, t, d)
```

```python
import functools
import math

import jax
import jax.numpy as jnp
from jax import lax
from jax.experimental import pallas as pl
from jax.experimental.pallas import tpu as pltpu

F32 = jnp.float32
BF16 = jnp.bfloat16
I32 = jnp.int32

D_MODEL = 1024
GDN_HEADS = 4
GDN_HEAD_DIM = 128
GDN_WIDTH = GDN_HEADS * GDN_HEAD_DIM
CONV_WIDTH = 4
GDN_CHUNK = 64
DSA_HEADS = 4
DSA_HEAD_DIM = 128
DSA_WIDTH = DSA_HEADS * DSA_HEAD_DIM
Q_RANK = 256
KV_RANK = 256
IDX_HEADS = 16
IDX_DIM = 64
IDX_TOPK_MAX = 256
REL_BUCKETS = 32
REL_MAX_DIST = 1024
PEER_HEADS = 8
PEER_KEYS = 128
PEER_KEY_DIM = 256
PEER_TOPK = 16
EPS = 1e-6

LANES = 128
SUBLANES = 8
VMEM_LIMIT = 56 * 1024 * 1024

SM_KIDX = 0
SM_WIDX = 64
SM_GA = 80
SM_GB = 84
INT_MIN = -2 ** 31
LOG2E = math.log2(math.e)


def _bdot(a, b):
    return jnp.dot(a.astype(BF16), b.astype(BF16), preferred_element_type=F32)


def _bdot_nt(a, b):
    return lax.dot_general(a.astype(BF16), b.astype(BF16), (((1,), (1,)), ((), ())),
                           preferred_element_type=F32)


def _bdot_tn(a, b):
    return lax.dot_general(a.astype(BF16), b.astype(BF16), (((0,), (0,)), ((), ())),
                           preferred_element_type=F32)


def _hdot(a, b):
    ah = a.astype(jnp.bfloat16)
    bh = b.astype(jnp.bfloat16)
    al = (a - ah.astype(F32)).astype(jnp.bfloat16)
    bl = (b - bh.astype(F32)).astype(jnp.bfloat16)
    dot = functools.partial(jnp.dot, preferred_element_type=F32)
    return dot(ah, bh) + (dot(ah, bl) + dot(al, bh))


def _rms(x, g):
    return x * lax.rsqrt(jnp.mean(x * x, axis=-1, keepdims=True) + EPS) * g


def _silu(x):
    return x * (1.0 / (1.0 + jnp.exp(-x)))


def _chunk_loop(n, unroll, body, carry):
    def group(j, c):
        for u in range(unroll):
            c = body(unroll * j + u, c)
        return c

    carry = lax.fori_loop(0, n // unroll, group, carry)
    return lax.fori_loop((n // unroll) * unroll, n, body, carry)


def _params(sem):
    return pltpu.CompilerParams(dimension_semantics=sem, vmem_limit_bytes=VMEM_LIMIT)


def _inproj_kernel(x_ref, g_ref, w_ref, gdn_ref, dsa_ref, small_ref):
    h = _rms(x_ref[...], g_ref[...])
    p = _bdot(h, w_ref[...])
    n_gdn = gdn_ref.shape[-1]
    n_dsa = dsa_ref.shape[-1]
    gdn_ref[...] = p[:, :n_gdn]
    dsa_ref[...] = p[:, n_gdn:n_gdn + n_dsa]
    small_ref[...] = p[:, n_gdn + n_dsa:]


def _inproj(x2, ln1_g, w_perm, tm):
    n = x2.shape[0]
    n_gdn = 4 * GDN_WIDTH
    n_dsa = Q_RANK + KV_RANK
    return pl.pallas_call(
        _inproj_kernel,
        grid=(n // tm,),
        in_specs=[pl.BlockSpec((tm, D_MODEL), lambda i: (i, 0)),
                  pl.BlockSpec((1, D_MODEL), lambda i: (0, 0)),
                  pl.BlockSpec(w_perm.shape, lambda i: (0, 0))],
        out_specs=[pl.BlockSpec((tm, n_gdn), lambda i: (i, 0)),
                   pl.BlockSpec((tm, n_dsa), lambda i: (i, 0)),
                   pl.BlockSpec((tm, LANES), lambda i: (i, 0))],
        out_shape=[jax.ShapeDtypeStruct((n, n_gdn), F32),
                   jax.ShapeDtypeStruct((n, n_dsa), F32),
                   jax.ShapeDtypeStruct((n, LANES), F32)],
        compiler_params=_params(("parallel",)),
    )(x2, ln1_g.reshape(1, D_MODEL), w_perm)


def _gdn_kernel(gdn_ref, small_ref, convw_ref, alog_ref, dtb_ref, ng_ref, o_ref,
                tail_ref, state_ref):
    tb = gdn_ref.shape[1]
    c = GDN_CHUNK
    n_chunks = tb // c
    w3 = 3 * GDN_WIDTH

    @pl.when(pl.program_id(1) == 0)
    def _():
        tail_ref[...] = jnp.zeros_like(tail_ref)
        state_ref[...] = jnp.zeros_like(state_ref)

    blk = gdn_ref[0]
    xin = blk[:, :w3]
    z = blk[:, w3:]
    tail = tail_ref[...]
    cw = convw_ref[...]
    acc = xin * cw[CONV_WIDTH - 1:CONV_WIDTH]
    rows8 = lax.broadcasted_iota(I32, (SUBLANES, w3), 0)
    for k in range(1, CONV_WIDTH):
        xk = pltpu.roll(xin, k, axis=0)
        fix = pltpu.roll(tail, k, axis=0)
        top = jnp.where(rows8 < k, fix, xk[:SUBLANES])
        xk = jnp.concatenate([top, xk[SUBLANES:]], axis=0)
        acc = acc + xk * cw[CONV_WIDTH - 1 - k:CONV_WIDTH - k]
    tail_ref[...] = xin[tb - SUBLANES:]
    qkv = _silu(acc)

    sm = small_ref[0]
    sp = sm + dtb_ref[...]
    softplus = jnp.maximum(sp, 0.0) + jnp.log(1.0 + jnp.exp(-jnp.abs(sp)))
    gl = -jnp.exp(alog_ref[...]) * softplus
    beta = 1.0 / (1.0 + jnp.exp(-sm))
    rin = lax.broadcasted_iota(I32, (tb, LANES), 0) % c
    s = 1
    while s < c:
        gl = gl + jnp.where(rin >= s, pltpu.roll(gl, s, axis=0), 0.0)
        s *= 2
    g_t = gl.T
    eg = jnp.exp(gl)

    ri = lax.broadcasted_iota(I32, (c, c), 0)
    ci = lax.broadcasted_iota(I32, (c, c), 1)
    tril = ri >= ci
    strict = ri > ci
    eye = (ri == ci).astype(F32)
    ng = ng_ref[...]
    scale = GDN_HEAD_DIM ** -0.5

    probs = [(ch, h) for ch in range(n_chunks) for h in range(GDN_HEADS)]
    qs, ks, gcols, egcols, decays, kbs, vbs, ms, t_invs = ({} for _ in range(9))
    for ch, h in probs:
        r0, l0 = ch * c, h * GDN_HEAD_DIM
        qh = qkv[r0:r0 + c, l0:l0 + GDN_HEAD_DIM]
        kh = qkv[r0:r0 + c, GDN_WIDTH + l0:GDN_WIDTH + l0 + GDN_HEAD_DIM]
        vh = qkv[r0:r0 + c, 2 * GDN_WIDTH + l0:2 * GDN_WIDTH + l0 + GDN_HEAD_DIM]
        qs[ch, h] = qh * lax.rsqrt(jnp.sum(qh * qh, axis=-1, keepdims=True) + EPS) * scale
        kh = kh * lax.rsqrt(jnp.sum(kh * kh, axis=-1, keepdims=True) + EPS)
        ks[ch, h] = kh
        gcol = gl[r0:r0 + c, SM_GA + h:SM_GA + h + 1]
        grow = g_t[SM_GA + h:SM_GA + h + 1, r0:r0 + c]
        gcols[ch, h] = gcol
        egcols[ch, h] = eg[r0:r0 + c, SM_GA + h:SM_GA + h + 1]
        bcol = beta[r0:r0 + c, SM_GB + h:SM_GB + h + 1]
        decays[ch, h] = jnp.exp(jnp.where(tril, gcol - grow, -jnp.inf))
        kbs[ch, h] = kh * bcol
        vbs[ch, h] = vh * bcol
    for p in probs:
        a_mat = jnp.where(strict, _bdot_nt(kbs[p], ks[p]) * decays[p], 0.0)
        ms[p] = -a_mat
        t_invs[p] = eye + ms[p]
    step = 1
    while step < c // 2:
        for p in probs:
            ms[p] = _hdot(ms[p], ms[p])
        for p in probs:
            t_invs[p] = t_invs[p] + _hdot(t_invs[p], ms[p])
        step *= 2
    us = {p: _bdot(t_invs[p], vbs[p]) for p in probs}
    ws = {p: _bdot(t_invs[p], kbs[p] * egcols[p]) for p in probs}
    attns = {p: _bdot_nt(qs[p], ks[p]) * decays[p] for p in probs}

    out_rows = []
    for ch in range(n_chunks):
        r0 = ch * c
        out_heads = []
        for h in range(GDN_HEADS):
            p = (ch, h)
            l0 = h * GDN_HEAD_DIM
            st = state_ref[h]
            v_new = us[p] - _bdot(ws[p], st)
            o = _bdot(qs[p] * egcols[p], st) + _bdot(attns[p], v_new)
            glast = gcols[p][c - 1:c]
            state_ref[h] = st * jnp.exp(glast) + _bdot_tn(ks[p] * jnp.exp(glast - gcols[p]), v_new)
            zh = z[r0:r0 + c, l0:l0 + GDN_HEAD_DIM]
            out_heads.append(_rms(o, ng) * _silu(zh))
        out_rows.append(jnp.concatenate(out_heads, axis=1))
    o_ref[0] = jnp.concatenate(out_rows, axis=0)


def _gdn(gdn3, small3, conv_w, a_log, dt_bias, norm_g, tb):
    b, t, _ = gdn3.shape
    alog_pad = jnp.zeros((1, LANES), F32).at[0, SM_GA:SM_GA + GDN_HEADS].set(a_log)
    dtb_pad = jnp.zeros((1, LANES), F32).at[0, SM_GA:SM_GA + GDN_HEADS].set(dt_bias)
    return pl.pallas_call(
        _gdn_kernel,
        grid=(b, t // tb),
        in_specs=[pl.BlockSpec((1, tb, 4 * GDN_WIDTH), lambda i, j: (i, j, 0)),
                  pl.BlockSpec((1, tb, LANES), lambda i, j: (i, j, 0)),
                  pl.BlockSpec((CONV_WIDTH, 3 * GDN_WIDTH), lambda i, j: (0, 0)),
                  pl.BlockSpec((1, LANES), lambda i, j: (0, 0)),
                  pl.BlockSpec((1, LANES), lambda i, j: (0, 0)),
                  pl.BlockSpec((1, GDN_HEAD_DIM), lambda i, j: (0, 0))],
        out_specs=pl.BlockSpec((1, tb, GDN_WIDTH), lambda i, j: (i, j, 0)),
        out_shape=jax.ShapeDtypeStruct((b, t, GDN_WIDTH), F32),
        scratch_shapes=[pltpu.VMEM((SUBLANES, 3 * GDN_WIDTH), F32),
                        pltpu.VMEM((GDN_HEADS, GDN_HEAD_DIM, GDN_HEAD_DIM), F32)],
        compiler_params=_params(("parallel", "arbitrary")),
    )(gdn3, small3, conv_w, alog_pad, dtb_pad, norm_g.reshape(1, GDN_HEAD_DIM))


def _dsa_prep_kernel(dsa_ref, small_ref, qg_ref, kvg_ref, wq_ref, wqi_t_ref, wk_ref, wv_t_ref,
                     lng_ref, lnb_ref, q_ref, qit_ref, k_ref, vt_ref, ki_ref, wt_ref):
    tm = dsa_ref.shape[1]
    blk = dsa_ref[0]
    cq = _rms(blk[:, :Q_RANK], qg_ref[...])
    ckv = _rms(blk[:, Q_RANK:], kvg_ref[...])
    q_ref[0] = _bdot(cq, wq_ref[...]).astype(BF16)
    k_ref[0] = _bdot(ckv, wk_ref[...]).astype(BF16)
    vt_ref[0] = _bdot_nt(wv_t_ref[...], ckv).astype(BF16)
    qit = _bdot_nt(wqi_t_ref[...], cq)
    nqb = tm // LANES
    cols = []
    for j in range(nqb):
        for h in range(IDX_HEADS):
            cols.append(qit[h * IDX_DIM:(h + 1) * IDX_DIM, j * LANES:(j + 1) * LANES])
    qit_ref[0] = jnp.concatenate(cols, axis=1).astype(BF16)
    sm = small_ref[0]
    kx = sm[:, SM_KIDX:SM_KIDX + IDX_DIM]
    mu = jnp.mean(kx, axis=-1, keepdims=True)
    xc = kx - mu
    kn = xc * lax.rsqrt(jnp.mean(xc * xc, axis=-1, keepdims=True) + EPS)
    ki_ref[0] = (kn * lng_ref[...] + lnb_ref[...]).astype(BF16)
    sm_t = sm.T
    wt_ref[0] = sm_t[SM_WIDX:SM_WIDX + IDX_HEADS] * (IDX_HEADS ** -0.5 * IDX_DIM ** -0.5)


def _dsa_prep(dsa3, small3, q_norm_g, kv_norm_g, w_q_up, w_qidx_up, w_kv_up, idx_ln_g, idx_ln_b, tm):
    b, t, _ = dsa3.shape
    wq = w_q_up.astype(BF16)
    wqi_t = w_qidx_up.T.astype(BF16)
    wkv = w_kv_up.reshape(KV_RANK, DSA_HEADS, 2, DSA_HEAD_DIM)
    wk = wkv[:, :, 0].reshape(KV_RANK, DSA_WIDTH).astype(BF16)
    wv_t = wkv[:, :, 1].reshape(KV_RANK, DSA_WIDTH).T.astype(BF16)
    full = lambda a: pl.BlockSpec(a.shape, lambda i, j: (0,) * a.ndim)
    qg = q_norm_g.reshape(1, Q_RANK)
    kvg = kv_norm_g.reshape(1, KV_RANK)
    lng = idx_ln_g.reshape(1, IDX_DIM)
    lnb = idx_ln_b.reshape(1, IDX_DIM)
    nqw = IDX_HEADS * LANES
    return pl.pallas_call(
        _dsa_prep_kernel,
        grid=(b, t // tm),
        in_specs=[pl.BlockSpec((1, tm, Q_RANK + KV_RANK), lambda i, j: (i, j, 0)),
                  pl.BlockSpec((1, tm, LANES), lambda i, j: (i, j, 0)),
                  full(qg), full(kvg), full(wq), full(wqi_t), full(wk), full(wv_t),
                  full(lng), full(lnb)],
        out_specs=[pl.BlockSpec((1, tm, DSA_WIDTH), lambda i, j: (i, j, 0)),
                   pl.BlockSpec((1, IDX_DIM, (tm // LANES) * nqw), lambda i, j: (i, 0, j)),
                   pl.BlockSpec((1, tm, DSA_WIDTH), lambda i, j: (i, j, 0)),
                   pl.BlockSpec((1, DSA_WIDTH, tm), lambda i, j: (i, 0, j)),
                   pl.BlockSpec((1, tm, IDX_DIM), lambda i, j: (i, j, 0)),
                   pl.BlockSpec((1, IDX_HEADS, tm), lambda i, j: (i, 0, j))],
        out_shape=[jax.ShapeDtypeStruct((b, t, DSA_WIDTH), BF16),
                   jax.ShapeDtypeStruct((b, IDX_DIM, (t // LANES) * nqw), BF16),
                   jax.ShapeDtypeStruct((b, t, DSA_WIDTH), BF16),
                   jax.ShapeDtypeStruct((b, DSA_WIDTH, t), BF16),
                   jax.ShapeDtypeStruct((b, t, IDX_DIM), BF16),
                   jax.ShapeDtypeStruct((b, IDX_HEADS, t), F32)],
        compiler_params=_params(("parallel", "parallel")),
    )(dsa3, small3, qg, kvg, wq, wqi_t, wk, wv_t, lng, lnb)


DSA_QB = 128
DSA_KC = 512
DSA_AC = DSA_KC
_REL_EXACT = REL_BUCKETS // 2
REL_SATURATION = math.ceil(_REL_EXACT * (REL_MAX_DIST / _REL_EXACT)
                           ** ((REL_BUCKETS - 1 - _REL_EXACT) / (REL_BUCKETS - _REL_EXACT)))
N_BIAS_TILES = -(-(REL_SATURATION + DSA_QB - 1) // DSA_QB) + 1


def _dsa_kernel(q_ref, qit_ref, wt_ref, ki_ref, k_ref, vt_ref, bias_ref, o_ref,
                keys_ref, *acc_refs, topk):
    i = pl.program_id(1)
    t0 = i * DSA_QB
    n_kc = (t0 + DSA_QB + DSA_KC - 1) // DSA_KC
    tq = t0 + lax.broadcasted_iota(I32, (1, DSA_QB), 1)
    qit = qit_ref[0]
    wt = wt_ref[0]

    def score_chunk(kc, carry):
        r0 = pl.multiple_of(kc * DSA_KC, DSA_KC)
        big = jnp.dot(ki_ref[0, pl.ds(r0, DSA_KC), :], qit, preferred_element_type=F32)
        sc = jnp.zeros((DSA_KC, DSA_QB), F32)
        for h in range(IDX_HEADS):
            sc = sc + jnp.maximum(big[:, h * DSA_QB:(h + 1) * DSA_QB], 0.0) * wt[h:h + 1]
        bits = pltpu.bitcast(sc, I32)
        key = bits ^ ((bits >> 31) & 0x7FFFFFFF)
        spos = r0 + lax.broadcasted_iota(I32, (DSA_KC, 1), 0)
        key = jnp.where(spos <= tq, key, INT_MIN)
        keys_ref[pl.ds(r0, DSA_KC), :] = key
        nonneg = (key >= 0).astype(I32)
        return carry + jnp.sum(nonneg.reshape(DSA_KC // SUBLANES, SUBLANES, DSA_QB), axis=0)

    c0 = jnp.sum(_chunk_loop(n_kc, 2, score_chunk, jnp.zeros((SUBLANES, DSA_QB), I32)),
                 axis=0, keepdims=True)

    def count_ge(cand):
        def body(kc, cnt):
            r0 = pl.multiple_of(kc * DSA_KC, DSA_KC)
            ge = (keys_ref[pl.ds(r0, DSA_KC), :] >= cand).astype(I32)
            return cnt + jnp.sum(ge.reshape(DSA_KC // SUBLANES, SUBLANES, DSA_QB), axis=0)
        cnt8 = lax.fori_loop(0, n_kc, body, jnp.zeros((SUBLANES, DSA_QB), I32))
        return jnp.sum(cnt8, axis=0, keepdims=True)

    ans = jnp.where(c0 >= topk, 0, jnp.full((1, DSA_QB), INT_MIN, I32))
    n_ge = jnp.where(c0 >= topk, c0, 0)

    def bit_pass(b, carry):
        ans, n_ge = carry
        cand = ans | (jnp.int32(1) << (30 - b))
        cc = count_ge(cand)
        take = cc >= topk
        return jnp.where(take, cand, ans), jnp.where(take, cc, n_ge)

    ans, n_ge = lax.fori_loop(0, 31, bit_pass, (ans, n_ge))
    thr = jnp.maximum(ans, INT_MIN + 1)


    @pl.when(jnp.max(n_ge) > topk)
    def _():
        need = topk - count_ge(thr + 1)

        def tied_before(pos):
            def body(kc, cnt):
                r0 = pl.multiple_of(kc * DSA_KC, DSA_KC)
                spos = r0 + lax.broadcasted_iota(I32, (DSA_KC, 1), 0)
                hit = ((keys_ref[pl.ds(r0, DSA_KC), :] == thr) & (spos < pos)).astype(I32)
                return cnt + jnp.sum(hit.reshape(DSA_KC // SUBLANES, SUBLANES, DSA_QB), axis=0)
            cnt8 = lax.fori_loop(0, n_kc, body, jnp.zeros((SUBLANES, DSA_QB), I32))
            return jnp.sum(cnt8, axis=0, keepdims=True)

        def pos_bit(b, lo):
            cand = lo + (jnp.int32(1) << (pos_bits - 1 - b))
            return jnp.where(tied_before(cand) < need, cand, lo)

        pos_bits = keys_ref.shape[0].bit_length()
        last = lax.fori_loop(0, pos_bits, pos_bit, jnp.zeros((1, DSA_QB), I32))

        def retire(kc, carry):
            r0 = pl.multiple_of(kc * DSA_KC, DSA_KC)
            spos = r0 + lax.broadcasted_iota(I32, (DSA_KC, 1), 0)
            key = keys_ref[pl.ds(r0, DSA_KC), :]
            keys_ref[pl.ds(r0, DSA_KC), :] = jnp.where((key == thr) & (spos > last), INT_MIN, key)
            return carry

        lax.fori_loop(0, n_kc, retire, 0)

    for h in range(DSA_HEADS):
        acc_refs[h][...] = jnp.zeros((DSA_HEAD_DIM, DSA_QB), F32)
    qb = q_ref[0]
    scale = DSA_HEAD_DIM ** -0.5 * LOG2E
    sub = DSA_AC // DSA_QB

    def attend(kc, carry):
        ms, ls = carry
        r0 = pl.multiple_of(kc * DSA_AC, DSA_AC)
        sel = keys_ref[pl.ds(r0, DSA_AC), :] >= thr
        tiles = [jnp.clip(i - (kc * sub + a), 0, N_BIAS_TILES - 1) for a in range(sub)]
        new_ms, new_ls = [], []
        for h in range(DSA_HEADS):
            l0 = h * DSA_HEAD_DIM
            s_t = lax.dot_general(k_ref[0, pl.ds(r0, DSA_AC), l0:l0 + DSA_HEAD_DIM],
                                  qb[:, l0:l0 + DSA_HEAD_DIM], (((1,), (1,)), ((), ())),
                                  preferred_element_type=F32)
            bias = jnp.concatenate([bias_ref[h, bt] for bt in tiles], axis=0)
            logit = jnp.where(sel, s_t * scale + bias, -jnp.inf)
            m_new = jnp.maximum(ms[h], jnp.max(logit, axis=0, keepdims=True))
            m_safe = jnp.where(m_new == -jnp.inf, 0.0, m_new)
            alpha = jnp.exp2(ms[h] - m_safe)
            p = jnp.exp2(logit - m_safe)
            new_ls.append(alpha * ls[h] + jnp.sum(p, axis=0, keepdims=True))
            new_ms.append(m_new)
            pv = jnp.dot(vt_ref[0, l0:l0 + DSA_HEAD_DIM, pl.ds(r0, DSA_AC)], p.astype(BF16),
                         preferred_element_type=F32)
            acc_refs[h][...] = acc_refs[h][...] * alpha + pv
        return tuple(new_ms), tuple(new_ls)

    m0 = tuple(jnp.full((1, DSA_QB), -jnp.inf, F32) for _ in range(DSA_HEADS))
    l0s = tuple(jnp.zeros((1, DSA_QB), F32) for _ in range(DSA_HEADS))
    _, ls = _chunk_loop(n_kc, 4, attend, (m0, l0s))
    o_ref[0] = jnp.concatenate([(acc_refs[h][...] / ls[h]).T for h in range(DSA_HEADS)], axis=1)


def _rel_bucket(dist):
    max_exact = REL_BUCKETS // 2
    n = jnp.maximum(dist, 0)
    nf = jnp.maximum(n, 1).astype(F32)
    large = max_exact + (jnp.log(nf / max_exact) / math.log(REL_MAX_DIST / max_exact)
                         * (REL_BUCKETS - max_exact)).astype(jnp.int32)
    large = jnp.minimum(large, REL_BUCKETS - 1)
    return jnp.where(n < max_exact, n, large)


def _bias_tiles(rel_bias):
    bt = DSA_QB
    span = 2 * bt
    dist = jnp.arange(-(bt - 1), N_BIAS_TILES * bt + 1)
    by_dist = (rel_bias[_rel_bucket(dist)].astype(F32) * LOG2E).T
    rows = jnp.stack([by_dist[:, j * bt:j * bt + span] for j in range(N_BIAS_TILES)], axis=1)
    rep = jnp.tile(rows, (1, 1, bt))[..., bt - 1:bt - 1 + bt * (span - 1)]
    return rep.reshape(DSA_HEADS, N_BIAS_TILES, bt, span - 1)[..., :bt]


def _dsa(q, qit, wt, ki, k, vt, bias, topk):
    b, t, _ = q.shape
    nqw = IDX_HEADS * DSA_QB
    t_pad = ((t + DSA_KC - 1) // DSA_KC) * DSA_KC
    kern = functools.partial(_dsa_kernel, topk=topk)
    return pl.pallas_call(
        kern,
        grid=(b, t // DSA_QB),
        in_specs=[pl.BlockSpec((1, DSA_QB, DSA_WIDTH), lambda i, j: (i, j, 0)),
                  pl.BlockSpec((1, IDX_DIM, nqw), lambda i, j: (i, 0, j)),
                  pl.BlockSpec((1, IDX_HEADS, DSA_QB), lambda i, j: (i, 0, j)),
                  pl.BlockSpec((1, t, IDX_DIM), lambda i, j: (i, 0, 0)),
                  pl.BlockSpec((1, t, DSA_WIDTH), lambda i, j: (i, 0, 0)),
                  pl.BlockSpec((1, DSA_WIDTH, t), lambda i, j: (i, 0, 0)),
                  pl.BlockSpec(bias.shape, lambda i, j: (0, 0, 0, 0))],
        out_specs=pl.BlockSpec((1, DSA_QB, DSA_WIDTH), lambda i, j: (i, j, 0)),
        out_shape=jax.ShapeDtypeStruct((b, t, DSA_WIDTH), F32),
        scratch_shapes=[pltpu.VMEM((t_pad, DSA_QB), I32)]
        + [pltpu.VMEM((DSA_HEAD_DIM, DSA_QB), F32)] * DSA_HEADS,
        compiler_params=_params(("parallel", "arbitrary")),
    )(q, qit, wt, ki, k, vt, bias)


def _top16_rows(s, n_rows):
    tm = s.shape[1]
    rid = lax.broadcasted_iota(I32, (n_rows, tm), 0)
    vals, idxs = [], []
    for _ in range(PEER_TOPK):
        m = jnp.max(s, axis=0, keepdims=True)
        ix = jnp.min(jnp.where(s == m, rid, n_rows), axis=0, keepdims=True)
        vals.append(m)
        idxs.append(ix)
        s = jnp.where(rid == ix, -jnp.inf, s)
    return jnp.concatenate(vals, axis=0), jnp.concatenate(idxs, axis=0)


def _mix_kernel(x_ref, oa_ref, ob_ref, wo_ref, g2_ref, wq_ref, sk1_ref, sk2_ref,
                x1_ref, h2_ref, eidx_ref, gates_ref):
    tm = x_ref.shape[0]
    o = jnp.concatenate([oa_ref[...], ob_ref[...]], axis=1)
    x1 = x_ref[...] + _bdot(o, wo_ref[...])
    x1_ref[...] = x1
    h2 = _rms(x1, g2_ref[...])
    for j in range(SUBLANES):
        h2_ref[:, j, :] = h2[:, j * LANES:(j + 1) * LANES]
    query = _bdot(h2, wq_ref[...])
    half = PEER_KEY_DIM // 2
    groups = [(0, PEER_TOPK)] + [(a, PEER_TOPK // (a + 1)) for a in range(1, SUBLANES)]
    n_cand = PEER_TOPK + SUBLANES * SUBLANES
    pid = lax.broadcasted_iota(I32, (n_cand, tm), 0)
    row8 = lax.broadcasted_iota(I32, (SUBLANES, tm), 0)
    e_rows, g_rows = [], []
    for h in range(PEER_HEADS):
        q1 = query[:, h * PEER_KEY_DIM:h * PEER_KEY_DIM + half]
        q2 = query[:, h * PEER_KEY_DIM + half:(h + 1) * PEER_KEY_DIM]
        s1 = _bdot_nt(sk1_ref[h], q1)
        s2 = _bdot_nt(sk2_ref[h], q2)
        v1, i1 = _top16_rows(s1, PEER_KEYS)
        v2, i2 = _top16_rows(s2, PEER_KEYS)
        cands, cidxs = [], []
        for a, nb in groups:
            rows = PEER_TOPK if a == 0 else SUBLANES
            cv = v1[a:a + 1] + v2[:rows]
            ci = i1[a:a + 1] * PEER_KEYS + i2[:rows]
            if nb < rows:
                cv = jnp.where(row8 < nb, cv, -jnp.inf)
            cands.append(cv)
            cidxs.append(ci)
        cands.append(v1[SUBLANES:] + v2[0:1])
        cidxs.append(i1[SUBLANES:] * PEER_KEYS + i2[0:1])
        cand = jnp.concatenate(cands, axis=0)
        cidx = jnp.concatenate(cidxs, axis=0)
        tops, eids = [], []
        for _ in range(PEER_TOPK):
            m = jnp.max(cand, axis=0, keepdims=True)
            pos = jnp.min(jnp.where(cand == m, pid, n_cand), axis=0, keepdims=True)
            hit = pid == pos
            eids.append(jnp.max(jnp.where(hit, cidx, -1), axis=0, keepdims=True))
            tops.append(m)
            cand = jnp.where(hit, -jnp.inf, cand)
        top_s = jnp.concatenate(tops, axis=0)
        ex = jnp.exp(top_s - top_s[0:1])
        g_rows.append(ex / jnp.sum(ex, axis=0, keepdims=True))
        e_rows.append(jnp.concatenate(eids, axis=0))
    gates_ref[...] = jnp.concatenate(g_rows, axis=0).T
    eidx_ref[...] = (jnp.concatenate(e_rows, axis=0) * PEER_SLAB).T


def _mix(x2, o_a, o_b, w_out, ln2_g, w_query, sk1, sk2, tm):
    n = x2.shape[0]
    wo = w_out.astype(BF16)
    wq = w_query.astype(BF16)
    sk1 = sk1.astype(BF16)
    sk2 = sk2.astype(BF16)
    nsel = PEER_HEADS * PEER_TOPK
    row = lambda w: pl.BlockSpec((tm, w), lambda i: (i, 0))
    full = lambda a: pl.BlockSpec(a.shape, lambda i: (0,) * a.ndim)
    g2 = ln2_g.reshape(1, D_MODEL)
    return pl.pallas_call(
        _mix_kernel,
        grid=(n // tm,),
        in_specs=[row(D_MODEL), row(GDN_WIDTH), row(DSA_WIDTH), full(wo), full(g2), full(wq),
                  full(sk1), full(sk2)],
        out_specs=[row(D_MODEL), pl.BlockSpec((tm, SUBLANES, LANES), lambda i: (i, 0, 0)), row(nsel), row(nsel)],
        out_shape=[jax.ShapeDtypeStruct((n, D_MODEL), F32),
                   jax.ShapeDtypeStruct((n, SUBLANES, LANES), F32),
                   jax.ShapeDtypeStruct((n, nsel), I32),
                   jax.ShapeDtypeStruct((n, nsel), F32)],
        compiler_params=_params(("parallel",)),
    )(x2, o_a, o_b, wo, g2, wq, sk1, sk2)


PEER_TB = 128
PEER_SLAB = 4
PEER_HALF = D_MODEL // 2
PEER_NSEL = PEER_HEADS * PEER_TOPK
PEER_COLS = PEER_NSEL * PEER_SLAB


def _pack_table(w):
    e = w.shape[0]
    bits = lax.bitcast_convert_type(w.astype(jnp.bfloat16), jnp.uint16).astype(jnp.uint32)
    words = bits[:, :PEER_HALF] | (bits[:, PEER_HALF:] << 16)
    return lax.bitcast_convert_type(words, I32).reshape(e * PEER_SLAB, LANES)


def _unpack_words(words):
    lo = lax.bitcast_convert_type(words << 16, F32)
    hi = lax.bitcast_convert_type(words & jnp.int32(-65536), F32)
    return lo, hi


def _split_bf16(x):
    head = x.astype(jnp.bfloat16).astype(F32)
    return head, x - head


def _gather_slabs(idx_ref, row, tab_ref):
    slabs = []
    for k in range(PEER_NSEL):
        start = pl.multiple_of(idx_ref[row, k], PEER_SLAB)
        slabs.append(tab_ref[pl.ds(start, PEER_SLAB), :])
    return jnp.concatenate(slabs, axis=0)


def _load_table(tab_hbm, tab_ref, sem):
    @pl.when(pl.program_id(0) == 0)
    def _():
        cp = pltpu.make_async_copy(tab_hbm, tab_ref, sem)
        cp.start()
        cp.wait()


def _index_copy(idx_hbm, block, idx_ref, sem):
    return pltpu.make_async_copy(idx_hbm.at[pl.ds(block * PEER_TB, PEER_TB)], idx_ref, sem)


def _peer_blocks(idx_hbm, idx_a, idx_b, sems, process):
    i = pl.program_id(0)
    last = pl.num_programs(0) - 1

    @pl.when(i == 0)
    def _():
        _index_copy(idx_hbm, 0, idx_a, sems.at[0]).start()

    _index_copy(idx_hbm, 2 * i + 1, idx_b, sems.at[1]).start()
    _index_copy(idx_hbm, 2 * i, idx_a, sems.at[0]).wait()
    process(0, idx_a)

    @pl.when(i < last)
    def _():
        _index_copy(idx_hbm, 2 * i + 2, idx_a, sems.at[0]).start()

    _index_copy(idx_hbm, 2 * i + 1, idx_b, sems.at[1]).wait()
    process(1, idx_b)


def _slab_mask():
    j = lax.broadcasted_iota(I32, (SUBLANES, PEER_COLS), 0)
    col = lax.broadcasted_iota(I32, (SUBLANES, PEER_COLS), 1)
    return (col % PEER_SLAB) == (j % PEER_SLAB)


def _peer_u_kernel(x3_ref, gates_ref, rep_ref, gsum_ref, idx_hbm, tab_hbm, c_ref,
                   tab_ref, idx_a, idx_b, z_ref, sem, isems):
    _load_table(tab_hbm, tab_ref, sem)
    mask = _slab_mask()

    def process(half, idx_ref):
        for t in range(PEER_TB):
            tok = half * PEER_TB + t
            xh, xt = _split_bf16(x3_ref[tok])
            a_lo = jnp.concatenate([xh[:PEER_SLAB], xt[:PEER_SLAB]], axis=0)
            a_hi = jnp.concatenate([xh[PEER_SLAB:], xt[PEER_SLAB:]], axis=0)
            lo, hi = _unpack_words(_gather_slabs(idx_ref, t, tab_ref))
            r = _bdot_nt(a_lo, lo) + _bdot_nt(a_hi, hi)
            z_ref[tok:tok + 1, :] = jnp.sum(jnp.where(mask, r, 0.0), axis=0, keepdims=True)

    _peer_blocks(idx_hbm, idx_a, idx_b, isems, process)

    gh, gt = _split_bf16(gates_ref[...])
    rep = rep_ref[...]
    gates_rep = _bdot(gh, rep) + _bdot(gt, rep)
    zh, zt = _split_bf16(z_ref[...])
    gsum = gsum_ref[...]
    z = _bdot(zh, gsum) + _bdot(zt, gsum)
    c_ref[...] = 0.5 * z * (1.0 + lax.erf(z * (2.0 ** -0.5))) * gates_rep


def _peer_v_kernel(c_in_ref, x1_ref, fg_ref, idx_hbm, tab_hbm, o_ref,
                   tab_ref, idx_a, idx_b, sem, isems):
    _load_table(tab_hbm, tab_ref, sem)
    mask = _slab_mask()
    rows = lax.broadcasted_iota(I32, (SUBLANES, PEER_COLS), 0)

    def process(half, idx_ref):
        for t in range(PEER_TB):
            tok = half * PEER_TB + t
            ch, ct = _split_bf16(c_in_ref[tok:tok + 1, :])
            c8 = jnp.where(mask, jnp.where(rows < PEER_SLAB, ch, ct), 0.0)
            lo, hi = _unpack_words(_gather_slabs(idx_ref, t, tab_ref))
            y_lo = _bdot(c8, lo)
            y_hi = _bdot(c8, hi)
            y = jnp.concatenate([y_lo[:PEER_SLAB] + y_lo[PEER_SLAB:],
                                 y_hi[:PEER_SLAB] + y_hi[PEER_SLAB:]], axis=0)
            for j in range(SUBLANES):
                o_ref[tok:tok + 1, j * LANES:(j + 1) * LANES] = y[j:j + 1]

    _peer_blocks(idx_hbm, idx_a, idx_b, isems, process)
    o_ref[...] = _rms(x1_ref[...] + o_ref[...], fg_ref[...])


def _peer(slab_row, x1, h3, gates, final_g, peer_u, peer_v):
    n = x1.shape[0]
    step = 2 * PEER_TB
    u_tab = _pack_table(peer_u)
    v_tab = _pack_table(peer_v)
    fg = final_g.reshape(1, D_MODEL)
    group = jnp.arange(PEER_COLS) // PEER_SLAB
    rep = (group[None, :] == jnp.arange(PEER_NSEL)[:, None]).astype(BF16)
    gsum = (group[None, :] == group[:, None]).astype(BF16)
    tok = pl.BlockSpec((step, D_MODEL), lambda i: (i, 0))
    sel = pl.BlockSpec((step, PEER_NSEL), lambda i: (i, 0))
    wide = pl.BlockSpec((step, PEER_COLS), lambda i: (i, 0))
    full = lambda a: pl.BlockSpec(a.shape, lambda i: (0,) * a.ndim)
    hbm = pl.BlockSpec(memory_space=pl.ANY)
    idx_buf = pltpu.SMEM((PEER_TB, PEER_NSEL), I32)
    coef = pl.pallas_call(
        _peer_u_kernel,
        grid=(n // step,),
        in_specs=[pl.BlockSpec((step, SUBLANES, LANES), lambda i: (i, 0, 0)), sel, full(rep), full(gsum), hbm, hbm],
        out_specs=wide,
        out_shape=jax.ShapeDtypeStruct((n, PEER_COLS), F32),
        scratch_shapes=[pltpu.VMEM(u_tab.shape, I32), idx_buf, idx_buf,
                        pltpu.VMEM((step, PEER_COLS), F32),
                        pltpu.SemaphoreType.DMA(()), pltpu.SemaphoreType.DMA((2,))],
        compiler_params=_params(("arbitrary",)),
    )(h3, gates, rep, gsum, slab_row, u_tab)
    return pl.pallas_call(
        _peer_v_kernel,
        grid=(n // step,),
        in_specs=[wide, tok, full(fg), hbm, hbm],
        out_specs=tok,
        out_shape=jax.ShapeDtypeStruct((n, D_MODEL), F32),
        scratch_shapes=[pltpu.VMEM(v_tab.shape, I32), idx_buf, idx_buf,
                        pltpu.SemaphoreType.DMA(()), pltpu.SemaphoreType.DMA((2,))],
        compiler_params=_params(("arbitrary",)),
    )(coef, x1, fg, slab_row, v_tab)


def _permute_w_in(w_in):
    o = [0]
    for s in (GDN_WIDTH, GDN_WIDTH, GDN_WIDTH, GDN_WIDTH, GDN_HEADS, GDN_HEADS, Q_RANK, KV_RANK,
              IDX_DIM, IDX_HEADS):
        o.append(o[-1] + s)
    gq_gz = w_in[:, o[0]:o[4]]
    ga = w_in[:, o[4]:o[5]]
    gb = w_in[:, o[5]:o[6]]
    cq_ckv = w_in[:, o[6]:o[8]]
    kidx = w_in[:, o[8]:o[9]]
    widx = w_in[:, o[9]:o[10]]
    pad = jnp.zeros((w_in.shape[0], LANES - (IDX_DIM + IDX_HEADS + 2 * GDN_HEADS)), w_in.dtype)
    return jnp.concatenate([gq_gz, cq_ckv, kidx, widx, ga, gb, pad], axis=1).astype(BF16)


def kernel(x, ln1_g, w_in, conv_w, a_log, dt_bias, gdn_norm_g, q_norm_g, kv_norm_g, w_q_up,
           w_qidx_up, w_kv_up, idx_ln_g, idx_ln_b, w_out, ln2_g, peer_w_query, peer_sub_keys_1,
           peer_sub_keys_2, peer_u, peer_v, rel_bias, final_g):
    b, t, d = x.shape
    n = b * t
    assert w_in.shape[0] == 1, "single-layer block only"
    l = 0
    topk = min(IDX_TOPK_MAX, t // 4)
    bias = _bias_tiles(rel_bias)
    xc = x.reshape(n, d)
    gdn, dsa, small = _inproj(xc, ln1_g[l], _permute_w_in(w_in[l]), tm=256)
    gdn3 = gdn.reshape(b, t, -1)
    dsa3 = dsa.reshape(b, t, -1)
    small3 = small.reshape(b, t, LANES)
    o_a = _gdn(gdn3, small3, conv_w[l], a_log[l], dt_bias[l], gdn_norm_g[l], tb=256)
    q, qit, k, vt, ki, wt = _dsa_prep(dsa3, small3, q_norm_g[l], kv_norm_g[l], w_q_up[l],
                                      w_qidx_up[l], w_kv_up[l], idx_ln_g[l], idx_ln_b[l], tm=512)
    o_b = _dsa(q, qit, wt, ki, k, vt, bias, topk)
    x1, h3, slab_row, gates = _mix(xc, o_a.reshape(n, -1), o_b.reshape(n, -1), w_out[l], ln2_g[l],
                                   peer_w_query[l], peer_sub_keys_1[l], peer_sub_keys_2[l], tm=256)
    out = _peer(slab_row, x1, h3, gates, final_g, peer_u[l], peer_v[l])
    return out.reshape(b, t, d)
```

```python
import functools
import math

import jax
import jax.numpy as jnp
from jax import lax
from jax.experimental import pallas as pl
from jax.experimental.pallas import tpu as pltpu

F32 = jnp.float32
BF16 = jnp.bfloat16
I32 = jnp.int32

D_MODEL = 1024
GDN_HEADS = 4
GDN_HEAD_DIM = 128
GDN_WIDTH = GDN_HEADS * GDN_HEAD_DIM
CONV_WIDTH = 4
GDN_CHUNK = 64
DSA_HEADS = 4
DSA_HEAD_DIM = 128
DSA_WIDTH = DSA_HEADS * DSA_HEAD_DIM
Q_RANK = 256
KV_RANK = 256
IDX_HEADS = 16
IDX_DIM = 64
IDX_TOPK_MAX = 256
REL_BUCKETS = 32
REL_MAX_DIST = 1024
PEER_HEADS = 8
PEER_KEYS = 128
PEER_KEY_DIM = 256
PEER_TOPK = 16
EPS = 1e-6

LANES = 128
SUBLANES = 8
VMEM_LIMIT = 56 * 1024 * 1024

SM_KIDX = 0
SM_WIDX = 64
SM_GA = 80
SM_GB = 84
INT_MIN = -2 ** 31
LOG2E = math.log2(math.e)


def _bdot(a, b):
    return jnp.dot(a.astype(BF16), b.astype(BF16), preferred_element_type=F32)


def _bdot_nt(a, b):
    return lax.dot_general(a.astype(BF16), b.astype(BF16), (((1,), (1,)), ((), ())),
                           preferred_element_type=F32)


def _bdot_tn(a, b):
    return lax.dot_general(a.astype(BF16), b.astype(BF16), (((0,), (0,)), ((), ())),
                           preferred_element_type=F32)


def _hdot(a, b):
    ah = a.astype(jnp.bfloat16)
    bh = b.astype(jnp.bfloat16)
    al = (a - ah.astype(F32)).astype(jnp.bfloat16)
    bl = (b - bh.astype(F32)).astype(jnp.bfloat16)
    dot = functools.partial(jnp.dot, preferred_element_type=F32)
    return dot(ah, bh) + (dot(ah, bl) + dot(al, bh))


def _rms(x, g):
    return x * lax.rsqrt(jnp.mean(x * x, axis=-1, keepdims=True) + EPS) * g


def _silu(x):
    return x * (1.0 / (1.0 + jnp.exp(-x)))


def _chunk_loop(n, unroll, body, carry):
    def group(j, c):
        for u in range(unroll):
            c = body(unroll * j + u, c)
        return c

    carry = lax.fori_loop(0, n // unroll, group, carry)
    return lax.fori_loop((n // unroll) * unroll, n, body, carry)


def _params(sem):
    return pltpu.CompilerParams(dimension_semantics=sem, vmem_limit_bytes=VMEM_LIMIT)


def _inproj_kernel(x_ref, g_ref, w_ref, gdn_ref, dsa_ref, small_ref):
    h = _rms(x_ref[...], g_ref[...])
    p = _bdot(h, w_ref[...])
    n_gdn = gdn_ref.shape[-1]
    n_dsa = dsa_ref.shape[-1]
    gdn_ref[...] = p[:, :n_gdn]
    dsa_ref[...] = p[:, n_gdn:n_gdn + n_dsa]
    small_ref[...] = p[:, n_gdn + n_dsa:]


def _inproj(x2, ln1_g, w_perm, tm):
    n = x2.shape[0]
    n_gdn = 4 * GDN_WIDTH
    n_dsa = Q_RANK + KV_RANK
    return pl.pallas_call(
        _inproj_kernel,
        grid=(n // tm,),
        in_specs=[pl.BlockSpec((tm, D_MODEL), lambda i: (i, 0)),
                  pl.BlockSpec((1, D_MODEL), lambda i: (0, 0)),
                  pl.BlockSpec(w_perm.shape, lambda i: (0, 0))],
        out_specs=[pl.BlockSpec((tm, n_gdn), lambda i: (i, 0)),
                   pl.BlockSpec((tm, n_dsa), lambda i: (i, 0)),
                   pl.BlockSpec((tm, LANES), lambda i: (i, 0))],
        out_shape=[jax.ShapeDtypeStruct((n, n_gdn), F32),
                   jax.ShapeDtypeStruct((n, n_dsa), F32),
                   jax.ShapeDtypeStruct((n, LANES), F32)],
        compiler_params=_params(("parallel",)),
    )(x2, ln1_g.reshape(1, D_MODEL), w_perm)


def _gdn_kernel(gdn_ref, small_ref, convw_ref, alog_ref, dtb_ref, ng_ref, o_ref,
                tail_ref, state_ref):
    tb = gdn_ref.shape[1]
    c = GDN_CHUNK
    n_chunks = tb // c
    w3 = 3 * GDN_WIDTH

    @pl.when(pl.program_id(1) == 0)
    def _():
        tail_ref[...] = jnp.zeros_like(tail_ref)
        state_ref[...] = jnp.zeros_like(state_ref)

    blk = gdn_ref[0]
    xin = blk[:, :w3]
    z = blk[:, w3:]
    tail = tail_ref[...]
    cw = convw_ref[...]
    acc = xin * cw[CONV_WIDTH - 1:CONV_WIDTH]
    rows8 = lax.broadcasted_iota(I32, (SUBLANES, w3), 0)
    for k in range(1, CONV_WIDTH):
        xk = pltpu.roll(xin, k, axis=0)
        fix = pltpu.roll(tail, k, axis=0)
        top = jnp.where(rows8 < k, fix, xk[:SUBLANES])
        xk = jnp.concatenate([top, xk[SUBLANES:]], axis=0)
        acc = acc + xk * cw[CONV_WIDTH - 1 - k:CONV_WIDTH - k]
    tail_ref[...] = xin[tb - SUBLANES:]
    qkv = _silu(acc)

    sm = small_ref[0]
    sp = sm + dtb_ref[...]
    softplus = jnp.maximum(sp, 0.0) + jnp.log(1.0 + jnp.exp(-jnp.abs(sp)))
    gl = -jnp.exp(alog_ref[...]) * softplus
    beta = 1.0 / (1.0 + jnp.exp(-sm))
    rin = lax.broadcasted_iota(I32, (tb, LANES), 0) % c
    s = 1
    while s < c:
        gl = gl + jnp.where(rin >= s, pltpu.roll(gl, s, axis=0), 0.0)
        s *= 2
    g_t = gl.T
    eg = jnp.exp(gl)

    ri = lax.broadcasted_iota(I32, (c, c), 0)
    ci = lax.broadcasted_iota(I32, (c, c), 1)
    tril = ri >= ci
    strict = ri > ci
    eye = (ri == ci).astype(F32)
    ng = ng_ref[...]
    scale = GDN_HEAD_DIM ** -0.5

    probs = [(ch, h) for ch in range(n_chunks) for h in range(GDN_HEADS)]
    qs, ks, gcols, egcols, decays, kbs, vbs, ms, t_invs = ({} for _ in range(9))
    for ch, h in probs:
        r0, l0 = ch * c, h * GDN_HEAD_DIM
        qh = qkv[r0:r0 + c, l0:l0 + GDN_HEAD_DIM]
        kh = qkv[r0:r0 + c, GDN_WIDTH + l0:GDN_WIDTH + l0 + GDN_HEAD_DIM]
        vh = qkv[r0:r0 + c, 2 * GDN_WIDTH + l0:2 * GDN_WIDTH + l0 + GDN_HEAD_DIM]
        qs[ch, h] = qh * lax.rsqrt(jnp.sum(qh * qh, axis=-1, keepdims=True) + EPS) * scale
        kh = kh * lax.rsqrt(jnp.sum(kh * kh, axis=-1, keepdims=True) + EPS)
        ks[ch, h] = kh
        gcol = gl[r0:r0 + c, SM_GA + h:SM_GA + h + 1]
        grow = g_t[SM_GA + h:SM_GA + h + 1, r0:r0 + c]
        gcols[ch, h] = gcol
        egcols[ch, h] = eg[r0:r0 + c, SM_GA + h:SM_GA + h + 1]
        bcol = beta[r0:r0 + c, SM_GB + h:SM_GB + h + 1]
        decays[ch, h] = jnp.exp(jnp.where(tril, gcol - grow, -jnp.inf))
        kbs[ch, h] = kh * bcol
        vbs[ch, h] = vh * bcol
    for p in probs:
        a_mat = jnp.where(strict, _bdot_nt(kbs[p], ks[p]) * decays[p], 0.0)
        ms[p] = -a_mat
        t_invs[p] = eye + ms[p]
    step = 1
    while step < c // 2:
        for p in probs:
            ms[p] = _hdot(ms[p], ms[p])
        for p in probs:
            t_invs[p] = t_invs[p] + _hdot(t_invs[p], ms[p])
        step *= 2
    us = {p: _bdot(t_invs[p], vbs[p]) for p in probs}
    ws = {p: _bdot(t_invs[p], kbs[p] * egcols[p]) for p in probs}
    attns = {p: _bdot_nt(qs[p], ks[p]) * decays[p] for p in probs}

    out_rows = []
    for ch in range(n_chunks):
        r0 = ch * c
        out_heads = []
        for h in range(GDN_HEADS):
            p = (ch, h)
            l0 = h * GDN_HEAD_DIM
            st = state_ref[h]
            v_new = us[p] - _bdot(ws[p], st)
            o = _bdot(qs[p] * egcols[p], st) + _bdot(attns[p], v_new)
            glast = gcols[p][c - 1:c]
            state_ref[h] = st * jnp.exp(glast) + _bdot_tn(ks[p] * jnp.exp(glast - gcols[p]), v_new)
            zh = z[r0:r0 + c, l0:l0 + GDN_HEAD_DIM]
            out_heads.append(_rms(o, ng) * _silu(zh))
        out_rows.append(jnp.concatenate(out_heads, axis=1))
    o_ref[0] = jnp.concatenate(out_rows, axis=0)


def _gdn(gdn3, small3, conv_w, a_log, dt_bias, norm_g, tb):
    b, t, _ = gdn3.shape
    alog_pad = jnp.zeros((1, LANES), F32).at[0, SM_GA:SM_GA + GDN_HEADS].set(a_log)
    dtb_pad = jnp.zeros((1, LANES), F32).at[0, SM_GA:SM_GA + GDN_HEADS].set(dt_bias)
    return pl.pallas_call(
        _gdn_kernel,
        grid=(b, t // tb),
        in_specs=[pl.BlockSpec((1, tb, 4 * GDN_WIDTH), lambda i, j: (i, j, 0)),
                  pl.BlockSpec((1, tb, LANES), lambda i, j: (i, j, 0)),
                  pl.BlockSpec((CONV_WIDTH, 3 * GDN_WIDTH), lambda i, j: (0, 0)),
                  pl.BlockSpec((1, LANES), lambda i, j: (0, 0)),
                  pl.BlockSpec((1, LANES), lambda i, j: (0, 0)),
                  pl.BlockSpec((1, GDN_HEAD_DIM), lambda i, j: (0, 0))],
        out_specs=pl.BlockSpec((1, tb, GDN_WIDTH), lambda i, j: (i, j, 0)),
        out_shape=jax.ShapeDtypeStruct((b, t, GDN_WIDTH), F32),
        scratch_shapes=[pltpu.VMEM((SUBLANES, 3 * GDN_WIDTH), F32),
                        pltpu.VMEM((GDN_HEADS, GDN_HEAD_DIM, GDN_HEAD_DIM), F32)],
        compiler_params=_params(("parallel", "arbitrary")),
    )(gdn3, small3, conv_w, alog_pad, dtb_pad, norm_g.reshape(1, GDN_HEAD_DIM))


def _dsa_prep_kernel(dsa_ref, small_ref, qg_ref, kvg_ref, wq_ref, wqi_t_ref, wk_ref, wv_t_ref,
                     lng_ref, lnb_ref, q_ref, qit_ref, k_ref, vt_ref, ki_ref, wt_ref):
    tm = dsa_ref.shape[1]
    blk = dsa_ref[0]
    cq = _rms(blk[:, :Q_RANK], qg_ref[...])
    ckv = _rms(blk[:, Q_RANK:], kvg_ref[...])
    q_ref[0] = _bdot(cq, wq_ref[...]).astype(BF16)
    k_ref[0] = _bdot(ckv, wk_ref[...]).astype(BF16)
    vt_ref[0] = _bdot_nt(wv_t_ref[...], ckv).astype(BF16)
    qit = _bdot_nt(wqi_t_ref[...], cq)
    nqb = tm // LANES
    cols = []
    for j in range(nqb):
        for h in range(IDX_HEADS):
            cols.append(qit[h * IDX_DIM:(h + 1) * IDX_DIM, j * LANES:(j + 1) * LANES])
    qit_ref[0] = jnp.concatenate(cols, axis=1).astype(BF16)
    sm = small_ref[0]
    kx = sm[:, SM_KIDX:SM_KIDX + IDX_DIM]
    mu = jnp.mean(kx, axis=-1, keepdims=True)
    xc = kx - mu
    kn = xc * lax.rsqrt(jnp.mean(xc * xc, axis=-1, keepdims=True) + EPS)
    ki_ref[0] = (kn * lng_ref[...] + lnb_ref[...]).astype(BF16)
    sm_t = sm.T
    wt_ref[0] = sm_t[SM_WIDX:SM_WIDX + IDX_HEADS] * (IDX_HEADS ** -0.5 * IDX_DIM ** -0.5)


def _dsa_prep(dsa3, small3, q_norm_g, kv_norm_g, w_q_up, w_qidx_up, w_kv_up, idx_ln_g, idx_ln_b, tm):
    b, t, _ = dsa3.shape
    wq = w_q_up.astype(BF16)
    wqi_t = w_qidx_up.T.astype(BF16)
    wkv = w_kv_up.reshape(KV_RANK, DSA_HEADS, 2, DSA_HEAD_DIM)
    wk = wkv[:, :, 0].reshape(KV_RANK, DSA_WIDTH).astype(BF16)
    wv_t = wkv[:, :, 1].reshape(KV_RANK, DSA_WIDTH).T.astype(BF16)
    full = lambda a: pl.BlockSpec(a.shape, lambda i, j: (0,) * a.ndim)
    qg = q_norm_g.reshape(1, Q_RANK)
    kvg = kv_norm_g.reshape(1, KV_RANK)
    lng = idx_ln_g.reshape(1, IDX_DIM)
    lnb = idx_ln_b.reshape(1, IDX_DIM)
    nqw = IDX_HEADS * LANES
    return pl.pallas_call(
        _dsa_prep_kernel,
        grid=(b, t // tm),
        in_specs=[pl.BlockSpec((1, tm, Q_RANK + KV_RANK), lambda i, j: (i, j, 0)),
                  pl.BlockSpec((1, tm, LANES), lambda i, j: (i, j, 0)),
                  full(qg), full(kvg), full(wq), full(wqi_t), full(wk), full(wv_t),
                  full(lng), full(lnb)],
        out_specs=[pl.BlockSpec((1, tm, DSA_WIDTH), lambda i, j: (i, j, 0)),
                   pl.BlockSpec((1, IDX_DIM, (tm // LANES) * nqw), lambda i, j: (i, 0, j)),
                   pl.BlockSpec((1, tm, DSA_WIDTH), lambda i, j: (i, j, 0)),
                   pl.BlockSpec((1, DSA_WIDTH, tm), lambda i, j: (i, 0, j)),
                   pl.BlockSpec((1, tm, IDX_DIM), lambda i, j: (i, j, 0)),
                   pl.BlockSpec((1, IDX_HEADS, tm), lambda i, j: (i, 0, j))],
        out_shape=[jax.ShapeDtypeStruct((b, t, DSA_WIDTH), BF16),
                   jax.ShapeDtypeStruct((b, IDX_DIM, (t // LANES) * nqw), BF16),
                   jax.ShapeDtypeStruct((b, t, DSA_WIDTH), BF16),
                   jax.ShapeDtypeStruct((b, DSA_WIDTH, t), BF16),
                   jax.ShapeDtypeStruct((b, t, IDX_DIM), BF16),
                   jax.ShapeDtypeStruct((b, IDX_HEADS, t), F32)],
        compiler_params=_params(("parallel", "parallel")),
    )(dsa3, small3, qg, kvg, wq, wqi_t, wk, wv_t, lng, lnb)


DSA_QB = 128
DSA_KC = 512
DSA_AC = DSA_KC
_REL_EXACT = REL_BUCKETS // 2
REL_SATURATION = math.ceil(_REL_EXACT * (REL_MAX_DIST / _REL_EXACT)
                           ** ((REL_BUCKETS - 1 - _REL_EXACT) / (REL_BUCKETS - _REL_EXACT)))
N_BIAS_TILES = -(-(REL_SATURATION + DSA_QB - 1) // DSA_QB) + 1


def _dsa_kernel(q_ref, qit_ref, wt_ref, ki_ref, k_ref, vt_ref, bias_ref, o_ref,
                keys_ref, *acc_refs, topk):
    i = pl.program_id(1)
    t0 = i * DSA_QB
    n_kc = (t0 + DSA_QB + DSA_KC - 1) // DSA_KC
    tq = t0 + lax.broadcasted_iota(I32, (1, DSA_QB), 1)
    qit = qit_ref[0]
    wt = wt_ref[0]

    def score_chunk(kc, carry):
        r0 = pl.multiple_of(kc * DSA_KC, DSA_KC)
        big = jnp.dot(ki_ref[0, pl.ds(r0, DSA_KC), :], qit, preferred_element_type=F32)
        sc = jnp.zeros((DSA_KC, DSA_QB), F32)
        for h in range(IDX_HEADS):
            sc = sc + jnp.maximum(big[:, h * DSA_QB:(h + 1) * DSA_QB], 0.0) * wt[h:h + 1]
        bits = pltpu.bitcast(sc, I32)
        key = bits ^ ((bits >> 31) & 0x7FFFFFFF)
        spos = r0 + lax.broadcasted_iota(I32, (DSA_KC, 1), 0)
        key = jnp.where(spos <= tq, key, INT_MIN)
        keys_ref[pl.ds(r0, DSA_KC), :] = key
        nonneg = (key >= 0).astype(I32)
        return carry + jnp.sum(nonneg.reshape(DSA_KC // SUBLANES, SUBLANES, DSA_QB), axis=0)

    c0 = jnp.sum(_chunk_loop(n_kc, 2, score_chunk, jnp.zeros((SUBLANES, DSA_QB), I32)),
                 axis=0, keepdims=True)

    def count_ge(cand):
        def body(kc, cnt):
            r0 = pl.multiple_of(kc * DSA_KC, DSA_KC)
            ge = (keys_ref[pl.ds(r0, DSA_KC), :] >= cand).astype(I32)
            return cnt + jnp.sum(ge.reshape(DSA_KC // SUBLANES, SUBLANES, DSA_QB), axis=0)
        cnt8 = lax.fori_loop(0, n_kc, body, jnp.zeros((SUBLANES, DSA_QB), I32))
        return jnp.sum(cnt8, axis=0, keepdims=True)

    ans = jnp.where(c0 >= topk, 0, jnp.full((1, DSA_QB), INT_MIN, I32))
    n_ge = jnp.where(c0 >= topk, c0, 0)

    def bit_pass(b, carry):
        ans, n_ge = carry
        cand = ans | (jnp.int32(1) << (30 - b))
        cc = count_ge(cand)
        take = cc >= topk
        return jnp.where(take, cand, ans), jnp.where(take, cc, n_ge)

    ans, n_ge = lax.fori_loop(0, 31, bit_pass, (ans, n_ge))
    thr = jnp.maximum(ans, INT_MIN + 1)


    @pl.when(jnp.max(n_ge) > topk)
    def _():
        need = topk - count_ge(thr + 1)

        def tied_before(pos):
            def body(kc, cnt):
                r0 = pl.multiple_of(kc * DSA_KC, DSA_KC)
                spos = r0 + lax.broadcasted_iota(I32, (DSA_KC, 1), 0)
                hit = ((keys_ref[pl.ds(r0, DSA_KC), :] == thr) & (spos < pos)).astype(I32)
                return cnt + jnp.sum(hit.reshape(DSA_KC // SUBLANES, SUBLANES, DSA_QB), axis=0)
            cnt8 = lax.fori_loop(0, n_kc, body, jnp.zeros((SUBLANES, DSA_QB), I32))
            return jnp.sum(cnt8, axis=0, keepdims=True)

        def pos_bit(b, lo):
            cand = lo + (jnp.int32(1) << (pos_bits - 1 - b))
            return jnp.where(tied_before(cand) < need, cand, lo)

        pos_bits = keys_ref.shape[0].bit_length()
        last = lax.fori_loop(0, pos_bits, pos_bit, jnp.zeros((1, DSA_QB), I32))

        def retire(kc, carry):
            r0 = pl.multiple_of(kc * DSA_KC, DSA_KC)
            spos = r0 + lax.broadcasted_iota(I32, (DSA_KC, 1), 0)
            key = keys_ref[pl.ds(r0, DSA_KC), :]
            keys_ref[pl.ds(r0, DSA_KC), :] = jnp.where((key == thr) & (spos > last), INT_MIN, key)
            return carry

        lax.fori_loop(0, n_kc, retire, 0)

    for h in range(DSA_HEADS):
        acc_refs[h][...] = jnp.zeros((DSA_HEAD_DIM, DSA_QB), F32)
    qb = q_ref[0]
    scale = DSA_HEAD_DIM ** -0.5 * LOG2E
    sub = DSA_AC // DSA_QB

    def attend(kc, carry):
        ms, ls = carry
        r0 = pl.multiple_of(kc * DSA_AC, DSA_AC)
        sel = keys_ref[pl.ds(r0, DSA_AC), :] >= thr
        tiles = [jnp.clip(i - (kc * sub + a), 0, N_BIAS_TILES - 1) for a in range(sub)]
        new_ms, new_ls = [], []
        for h in range(DSA_HEADS):
            l0 = h * DSA_HEAD_DIM
            s_t = lax.dot_general(k_ref[0, pl.ds(r0, DSA_AC), l0:l0 + DSA_HEAD_DIM],
                                  qb[:, l0:l0 + DSA_HEAD_DIM], (((1,), (1,)), ((), ())),
                                  preferred_element_type=F32)
            bias = jnp.concatenate([bias_ref[h, bt] for bt in tiles], axis=0)
            logit = jnp.where(sel, s_t * scale + bias, -jnp.inf)
            m_new = jnp.maximum(ms[h], jnp.max(logit, axis=0, keepdims=True))
            m_safe = jnp.where(m_new == -jnp.inf, 0.0, m_new)
            alpha = jnp.exp2(ms[h] - m_safe)
            p = jnp.exp2(logit - m_safe)
            new_ls.append(alpha * ls[h] + jnp.sum(p, axis=0, keepdims=True))
            new_ms.append(m_new)
            pv = jnp.dot(vt_ref[0, l0:l0 + DSA_HEAD_DIM, pl.ds(r0, DSA_AC)], p.astype(BF16),
                         preferred_element_type=F32)
            acc_refs[h][...] = acc_refs[h][...] * alpha + pv
        return tuple(new_ms), tuple(new_ls)

    m0 = tuple(jnp.full((1, DSA_QB), -jnp.inf, F32) for _ in range(DSA_HEADS))
    l0s = tuple(jnp.zeros((1, DSA_QB), F32) for _ in range(DSA_HEADS))
    _, ls = _chunk_loop(n_kc, 4, attend, (m0, l0s))
    o_ref[0] = jnp.concatenate([(acc_refs[h][...] / ls[h]).T for h in range(DSA_HEADS)], axis=1)


def _rel_bucket(dist):
    max_exact = REL_BUCKETS // 2
    n = jnp.maximum(dist, 0)
    nf = jnp.maximum(n, 1).astype(F32)
    large = max_exact + (jnp.log(nf / max_exact) / math.log(REL_MAX_DIST / max_exact)
                         * (REL_BUCKETS - max_exact)).astype(jnp.int32)
    large = jnp.minimum(large, REL_BUCKETS - 1)
    return jnp.where(n < max_exact, n, large)


def _bias_tiles(rel_bias):
    bt = DSA_QB
    span = 2 * bt
    dist = jnp.arange(-(bt - 1), N_BIAS_TILES * bt + 1)
    by_dist = (rel_bias[_rel_bucket(dist)].astype(F32) * LOG2E).T
    rows = jnp.stack([by_dist[:, j * bt:j * bt + span] for j in range(N_BIAS_TILES)], axis=1)
    rep = jnp.tile(rows, (1, 1, bt))[..., bt - 1:bt - 1 + bt * (span - 1)]
    return rep.reshape(DSA_HEADS, N_BIAS_TILES, bt, span - 1)[..., :bt]


def _dsa(q, qit, wt, ki, k, vt, bias, topk):
    b, t, _ = q.shape
    nqw = IDX_HEADS * DSA_QB
    t_pad = ((t + DSA_KC - 1) // DSA_KC) * DSA_KC
    kern = functools.partial(_dsa_kernel, topk=topk)
    return pl.pallas_call(
        kern,
        grid=(b, t // DSA_QB),
        in_specs=[pl.BlockSpec((1, DSA_QB, DSA_WIDTH), lambda i, j: (i, j, 0)),
                  pl.BlockSpec((1, IDX_DIM, nqw), lambda i, j: (i, 0, j)),
                  pl.BlockSpec((1, IDX_HEADS, DSA_QB), lambda i, j: (i, 0, j)),
                  pl.BlockSpec((1, t, IDX_DIM), lambda i, j: (i, 0, 0)),
                  pl.BlockSpec((1, t, DSA_WIDTH), lambda i, j: (i, 0, 0)),
                  pl.BlockSpec((1, DSA_WIDTH, t), lambda i, j: (i, 0, 0)),
                  pl.BlockSpec(bias.shape, lambda i, j: (0, 0, 0, 0))],
        out_specs=pl.BlockSpec((1, DSA_QB, DSA_WIDTH), lambda i, j: (i, j, 0)),
        out_shape=jax.ShapeDtypeStruct((b, t, DSA_WIDTH), F32),
        scratch_shapes=[pltpu.VMEM((t_pad, DSA_QB), I32)]
        + [pltpu.VMEM((DSA_HEAD_DIM, DSA_QB), F32)] * DSA_HEADS,
        compiler_params=_params(("parallel", "arbitrary")),
    )(q, qit, wt, ki, k, vt, bias)


def _top16_rows(s, n_rows):
    tm = s.shape[1]
    rid = lax.broadcasted_iota(I32, (n_rows, tm), 0)
    vals, idxs = [], []
    for _ in range(PEER_TOPK):
        m = jnp.max(s, axis=0, keepdims=True)
        ix = jnp.min(jnp.where(s == m, rid, n_rows), axis=0, keepdims=True)
        vals.append(m)
        idxs.append(ix)
        s = jnp.where(rid == ix, -jnp.inf, s)
    return jnp.concatenate(vals, axis=0), jnp.concatenate(idxs, axis=0)


def _mix_kernel(x_ref, oa_ref, ob_ref, wo_ref, g2_ref, wq_ref, sk1_ref, sk2_ref,
                x1_ref, h2_ref, eidx_ref, gates_ref):
    tm = x_ref.shape[0]
    o = jnp.concatenate([oa_ref[...], ob_ref[...]], axis=1)
    x1 = x_ref[...] + _bdot(o, wo_ref[...])
    x1_ref[...] = x1
    h2 = _rms(x1, g2_ref[...])
    for j in range(SUBLANES):
        h2_ref[:, j, :] = h2[:, j * LANES:(j + 1) * LANES]
    query = _bdot(h2, wq_ref[...])
    half = PEER_KEY_DIM // 2
    groups = [(0, PEER_TOPK)] + [(a, PEER_TOPK // (a + 1)) for a in range(1, SUBLANES)]
    n_cand = PEER_TOPK + SUBLANES * SUBLANES
    pid = lax.broadcasted_iota(I32, (n_cand, tm), 0)
    row8 = lax.broadcasted_iota(I32, (SUBLANES, tm), 0)
    e_rows, g_rows = [], []
    for h in range(PEER_HEADS):
        q1 = query[:, h * PEER_KEY_DIM:h * PEER_KEY_DIM + half]
        q2 = query[:, h * PEER_KEY_DIM + half:(h + 1) * PEER_KEY_DIM]
        s1 = _bdot_nt(sk1_ref[h], q1)
        s2 = _bdot_nt(sk2_ref[h], q2)
        v1, i1 = _top16_rows(s1, PEER_KEYS)
        v2, i2 = _top16_rows(s2, PEER_KEYS)
        cands, cidxs = [], []
        for a, nb in groups:
            rows = PEER_TOPK if a == 0 else SUBLANES
            cv = v1[a:a + 1] + v2[:rows]
            ci = i1[a:a + 1] * PEER_KEYS + i2[:rows]
            if nb < rows:
                cv = jnp.where(row8 < nb, cv, -jnp.inf)
            cands.append(cv)
            cidxs.append(ci)
        cands.append(v1[SUBLANES:] + v2[0:1])
        cidxs.append(i1[SUBLANES:] * PEER_KEYS + i2[0:1])
        cand = jnp.concatenate(cands, axis=0)
        cidx = jnp.concatenate(cidxs, axis=0)
        tops, eids = [], []
        for _ in range(PEER_TOPK):
            m = jnp.max(cand, axis=0, keepdims=True)
            pos = jnp.min(jnp.where(cand == m, pid, n_cand), axis=0, keepdims=True)
            hit = pid == pos
            eids.append(jnp.max(jnp.where(hit, cidx, -1), axis=0, keepdims=True))
            tops.append(m)
            cand = jnp.where(hit, -jnp.inf, cand)
        top_s = jnp.concatenate(tops, axis=0)
        ex = jnp.exp(top_s - top_s[0:1])
        g_rows.append(ex / jnp.sum(ex, axis=0, keepdims=True))
        e_rows.append(jnp.concatenate(eids, axis=0))
    gates_ref[...] = jnp.concatenate(g_rows, axis=0).T
    eidx_ref[...] = (jnp.concatenate(e_rows, axis=0) * PEER_SLAB).T


def _mix(x2, o_a, o_b, w_out, ln2_g, w_query, sk1, sk2, tm):
    n = x2.shape[0]
    wo = w_out.astype(BF16)
    wq = w_query.astype(BF16)
    sk1 = sk1.astype(BF16)
    sk2 = sk2.astype(BF16)
    nsel = PEER_HEADS * PEER_TOPK
    row = lambda w: pl.BlockSpec((tm, w), lambda i: (i, 0))
    full = lambda a: pl.BlockSpec(a.shape, lambda i: (0,) * a.ndim)
    g2 = ln2_g.reshape(1, D_MODEL)
    return pl.pallas_call(
        _mix_kernel,
        grid=(n // tm,),
        in_specs=[row(D_MODEL), row(GDN_WIDTH), row(DSA_WIDTH), full(wo), full(g2), full(wq),
                  full(sk1), full(sk2)],
        out_specs=[row(D_MODEL), pl.BlockSpec((tm, SUBLANES, LANES), lambda i: (i, 0, 0)), row(nsel), row(nsel)],
        out_shape=[jax.ShapeDtypeStruct((n, D_MODEL), F32),
                   jax.ShapeDtypeStruct((n, SUBLANES, LANES), F32),
                   jax.ShapeDtypeStruct((n, nsel), I32),
                   jax.ShapeDtypeStruct((n, nsel), F32)],
        compiler_params=_params(("parallel",)),
    )(x2, o_a, o_b, wo, g2, wq, sk1, sk2)


PEER_TB = 128
PEER_SLAB = 4
PEER_HALF = D_MODEL // 2
PEER_NSEL = PEER_HEADS * PEER_TOPK
PEER_COLS = PEER_NSEL * PEER_SLAB


def _unpack_words(words):
    lo = lax.bitcast_convert_type(words << 16, F32)
    hi = lax.bitcast_convert_type(words & jnp.int32(-65536), F32)
    return lo, hi


def _split_bf16(x):
    head = x.astype(jnp.bfloat16).astype(F32)
    return head, x - head


def _gather_slabs(idx_ref, row, tab_ref):
    slabs = []
    for k in range(PEER_NSEL):
        start = pl.multiple_of(idx_ref[row, k], PEER_SLAB)
        slabs.append(tab_ref[pl.ds(start, PEER_SLAB), :])
    return jnp.concatenate(slabs, axis=0)


PACK_ROWS = 256


def _load_table(w_hbm, tab_ref, stage_ref, sems):
    n_chunks = w_hbm.shape[0] // PACK_ROWS

    def chunk_copy(c, slot):
        return pltpu.make_async_copy(w_hbm.at[pl.ds(c * PACK_ROWS, PACK_ROWS)], stage_ref.at[slot],
                                     sems.at[slot])

    @pl.when(pl.program_id(0) == 0)
    def _():
        chunk_copy(0, 0).start()

        def pack(c, carry):
            slot = c % 2

            @pl.when(c + 1 < n_chunks)
            def _():
                chunk_copy(c + 1, 1 - slot).start()

            chunk_copy(c, slot).wait()
            w = stage_ref[slot]
            lo = lax.bitcast_convert_type(w[:, :PEER_HALF].astype(jnp.bfloat16).astype(F32), I32)
            hi = lax.bitcast_convert_type(w[:, PEER_HALF:].astype(jnp.bfloat16).astype(F32), I32)
            words = lax.shift_right_logical(lo, 16) | (hi & jnp.int32(-65536))
            for r in range(PEER_SLAB):
                tab_ref[pl.ds(c * (PACK_ROWS * PEER_SLAB) + r, PACK_ROWS, stride=PEER_SLAB), :] = (
                    words[:, r * LANES:(r + 1) * LANES])
            return carry

        lax.fori_loop(0, n_chunks, pack, 0)


def _index_copy(idx_hbm, block, idx_ref, sem):
    return pltpu.make_async_copy(idx_hbm.at[pl.ds(block * PEER_TB, PEER_TB)], idx_ref, sem)


def _peer_blocks(idx_hbm, idx_a, idx_b, sems, process):
    i = pl.program_id(0)
    last = pl.num_programs(0) - 1

    @pl.when(i == 0)
    def _():
        _index_copy(idx_hbm, 0, idx_a, sems.at[0]).start()

    _index_copy(idx_hbm, 2 * i + 1, idx_b, sems.at[1]).start()
    _index_copy(idx_hbm, 2 * i, idx_a, sems.at[0]).wait()
    process(0, idx_a)

    @pl.when(i < last)
    def _():
        _index_copy(idx_hbm, 2 * i + 2, idx_a, sems.at[0]).start()

    _index_copy(idx_hbm, 2 * i + 1, idx_b, sems.at[1]).wait()
    process(1, idx_b)


def _slab_mask():
    j = lax.broadcasted_iota(I32, (SUBLANES, PEER_COLS), 0)
    col = lax.broadcasted_iota(I32, (SUBLANES, PEER_COLS), 1)
    return (col % PEER_SLAB) == (j % PEER_SLAB)


def _peer_u_kernel(x3_ref, gates_ref, rep_ref, gsum_ref, idx_hbm, tab_hbm, c_ref,
                   tab_ref, stage_ref, idx_a, idx_b, z_ref, sems, isems):
    _load_table(tab_hbm, tab_ref, stage_ref, sems)
    mask = _slab_mask()

    def process(half, idx_ref):
        for t in range(PEER_TB):
            tok = half * PEER_TB + t
            xh, xt = _split_bf16(x3_ref[tok])
            a_lo = jnp.concatenate([xh[:PEER_SLAB], xt[:PEER_SLAB]], axis=0)
            a_hi = jnp.concatenate([xh[PEER_SLAB:], xt[PEER_SLAB:]], axis=0)
            lo, hi = _unpack_words(_gather_slabs(idx_ref, t, tab_ref))
            r = _bdot_nt(a_lo, lo) + _bdot_nt(a_hi, hi)
            z_ref[tok:tok + 1, :] = jnp.sum(jnp.where(mask, r, 0.0), axis=0, keepdims=True)

    _peer_blocks(idx_hbm, idx_a, idx_b, isems, process)

    gh, gt = _split_bf16(gates_ref[...])
    rep = rep_ref[...]
    gates_rep = _bdot(gh, rep) + _bdot(gt, rep)
    zh, zt = _split_bf16(z_ref[...])
    gsum = gsum_ref[...]
    z = _bdot(zh, gsum) + _bdot(zt, gsum)
    c_ref[...] = 0.5 * z * (1.0 + lax.erf(z * (2.0 ** -0.5))) * gates_rep


def _peer_v_kernel(c_in_ref, x1_ref, fg_ref, idx_hbm, tab_hbm, o_ref,
                   tab_ref, stage_ref, idx_a, idx_b, sems, isems):
    _load_table(tab_hbm, tab_ref, stage_ref, sems)
    mask = _slab_mask()
    rows = lax.broadcasted_iota(I32, (SUBLANES, PEER_COLS), 0)

    def process(half, idx_ref):
        for t in range(PEER_TB):
            tok = half * PEER_TB + t
            ch, ct = _split_bf16(c_in_ref[tok:tok + 1, :])
            c8 = jnp.where(mask, jnp.where(rows < PEER_SLAB, ch, ct), 0.0)
            lo, hi = _unpack_words(_gather_slabs(idx_ref, t, tab_ref))
            y_lo = _bdot(c8, lo)
            y_hi = _bdot(c8, hi)
            y = jnp.concatenate([y_lo[:PEER_SLAB] + y_lo[PEER_SLAB:],
                                 y_hi[:PEER_SLAB] + y_hi[PEER_SLAB:]], axis=0)
            for j in range(SUBLANES):
                o_ref[tok:tok + 1, j * LANES:(j + 1) * LANES] = y[j:j + 1]

    _peer_blocks(idx_hbm, idx_a, idx_b, isems, process)
    o_ref[...] = _rms(x1_ref[...] + o_ref[...], fg_ref[...])


def _peer(slab_row, x1, h3, gates, final_g, peer_u, peer_v):
    n = x1.shape[0]
    step = 2 * PEER_TB
    table = pltpu.VMEM((peer_u.shape[0] * PEER_SLAB, LANES), I32)
    staging = pltpu.VMEM((2, PACK_ROWS, D_MODEL), F32)
    fg = final_g.reshape(1, D_MODEL)
    group = jnp.arange(PEER_COLS) // PEER_SLAB
    rep = (group[None, :] == jnp.arange(PEER_NSEL)[:, None]).astype(BF16)
    gsum = (group[None, :] == group[:, None]).astype(BF16)
    tok = pl.BlockSpec((step, D_MODEL), lambda i: (i, 0))
    sel = pl.BlockSpec((step, PEER_NSEL), lambda i: (i, 0))
    wide = pl.BlockSpec((step, PEER_COLS), lambda i: (i, 0))
    full = lambda a: pl.BlockSpec(a.shape, lambda i: (0,) * a.ndim)
    hbm = pl.BlockSpec(memory_space=pl.ANY)
    idx_buf = pltpu.SMEM((PEER_TB, PEER_NSEL), I32)
    coef = pl.pallas_call(
        _peer_u_kernel,
        grid=(n // step,),
        in_specs=[pl.BlockSpec((step, SUBLANES, LANES), lambda i: (i, 0, 0)), sel, full(rep), full(gsum), hbm, hbm],
        out_specs=wide,
        out_shape=jax.ShapeDtypeStruct((n, PEER_COLS), F32),
        scratch_shapes=[table, staging, idx_buf, idx_buf,
                        pltpu.VMEM((step, PEER_COLS), F32),
                        pltpu.SemaphoreType.DMA((2,)), pltpu.SemaphoreType.DMA((2,))],
        compiler_params=_params(("arbitrary",)),
    )(h3, gates, rep, gsum, slab_row, peer_u)
    return pl.pallas_call(
        _peer_v_kernel,
        grid=(n // step,),
        in_specs=[wide, tok, full(fg), hbm, hbm],
        out_specs=tok,
        out_shape=jax.ShapeDtypeStruct((n, D_MODEL), F32),
        scratch_shapes=[table, staging, idx_buf, idx_buf,
                        pltpu.SemaphoreType.DMA((2,)), pltpu.SemaphoreType.DMA((2,))],
        compiler_params=_params(("arbitrary",)),
    )(coef, x1, fg, slab_row, peer_v)


def _permute_w_in(w_in):
    o = [0]
    for s in (GDN_WIDTH, GDN_WIDTH, GDN_WIDTH, GDN_WIDTH, GDN_HEADS, GDN_HEADS, Q_RANK, KV_RANK,
              IDX_DIM, IDX_HEADS):
        o.append(o[-1] + s)
    gq_gz = w_in[:, o[0]:o[4]]
    ga = w_in[:, o[4]:o[5]]
    gb = w_in[:, o[5]:o[6]]
    cq_ckv = w_in[:, o[6]:o[8]]
    kidx = w_in[:, o[8]:o[9]]
    widx = w_in[:, o[9]:o[10]]
    pad = jnp.zeros((w_in.shape[0], LANES - (IDX_DIM + IDX_HEADS + 2 * GDN_HEADS)), w_in.dtype)
    return jnp.concatenate([gq_gz, cq_ckv, kidx, widx, ga, gb, pad], axis=1).astype(BF16)


def kernel(x, ln1_g, w_in, conv_w, a_log, dt_bias, gdn_norm_g, q_norm_g, kv_norm_g, w_q_up,
           w_qidx_up, w_kv_up, idx_ln_g, idx_ln_b, w_out, ln2_g, peer_w_query, peer_sub_keys_1,
           peer_sub_keys_2, peer_u, peer_v, rel_bias, final_g):
    b, t, d = x.shape
    n = b * t
    assert w_in.shape[0] == 1, "single-layer block only"
    l = 0
    topk = min(IDX_TOPK_MAX, t // 4)
    bias = _bias_tiles(rel_bias)
    xc = x.reshape(n, d)
    gdn, dsa, small = _inproj(xc, ln1_g[l], _permute_w_in(w_in[l]), tm=256)
    gdn3 = gdn.reshape(b, t, -1)
    dsa3 = dsa.reshape(b, t, -1)
    small3 = small.reshape(b, t, LANES)
    o_a = _gdn(gdn3, small3, conv_w[l], a_log[l], dt_bias[l], gdn_norm_g[l], tb=256)
    q, qit, k, vt, ki, wt = _dsa_prep(dsa3, small3, q_norm_g[l], kv_norm_g[l], w_q_up[l],
                                      w_qidx_up[l], w_kv_up[l], idx_ln_g[l], idx_ln_b[l], tm=512)
    o_b = _dsa(q, qit, wt, ki, k, vt, bias, topk)
    x1, h3, slab_row, gates = _mix(xc, o_a.reshape(n, -1), o_b.reshape(n, -1), w_out[l], ln2_g[l],
                                   peer_w_query[l], peer_sub_keys_1[l], peer_sub_keys_2[l], tm=256)
    out = _peer(slab_row, x1, h3, gates, final_g, peer_u[l], peer_v[l])
    return out.reshape(b, t, d)
```
